```python
import math
import jax
import jax.numpy as jnp
from jax import lax
import numpy as np


D_MODEL = 2048
BATCH = 1
SEQ = 16384
DEPTH = 2

GRID_W = 64
CTX_LEN = 256

CONV_WIDTH = 512
CONV_TAPS = 31
MLA_HEADS = 8
QK_NOPE_DIM = 128
QK_ROPE_DIM = 64
V_HEAD_DIM = 128
Q_LORA_RANK = 512
KV_LORA_RANK = 256
MLA_WIDTH = MLA_HEADS * V_HEAD_DIM
HYENA_WIDTH = 512
HYENA_ORDER = 2
HYENA_SHORT_TAPS = 3
HYENA_EMB_DIM = 33
HYENA_BANDS = (HYENA_EMB_DIM - 1) // 2
HYENA_FILTER_HIDDEN = 64
HYENA_FAST_DECAY_PCT = 0.3
HYENA_SLOW_DECAY_PCT = 1.5
HYENA_DECAY_TARGET = 1e-2
MIX_WIDTH = CONV_WIDTH + MLA_WIDTH + HYENA_WIDTH
D_FF = 4 * D_MODEL
N_MOD = 6
ROPE_BASE = 10000.0
NORM_EPS = 1e-6
ATTN_BLOCK = 128
IN_A = 2 * CONV_WIDTH
IN_B = Q_LORA_RANK + KV_LORA_RANK + QK_ROPE_DIM
IN_C = (HYENA_ORDER + 1) * HYENA_WIDTH
IN_WIDTH = IN_A + IN_B + IN_C

kernel_name = 'hybrid_conv_mla_hyena_dit_trunk'


def rms_norm(x, g):
    xf = x.astype(jnp.float32)
    y = xf * lax.rsqrt(jnp.mean(xf * xf, axis=-1, keepdims=True) + NORM_EPS)
    return (y * g.astype(jnp.float32)).astype(x.dtype)


def layer_norm(x, g, b):
    xf = x.astype(jnp.float32)
    mu = jnp.mean(xf, axis=-1, keepdims=True)
    var = jnp.mean(jnp.square(xf - mu), axis=-1, keepdims=True)
    y = (xf - mu) * lax.rsqrt(var + NORM_EPS) * g.astype(jnp.float32) + b.astype(jnp.float32)
    return y.astype(x.dtype)


def modulate(h, shift, scale):
    return h * (1.0 + scale) + shift


def depthwise_conv(x, w, b):
    taps = w.shape[0]
    pad = (taps - 1) // 2
    y = lax.conv_general_dilated(
        x, w[:, None, :].astype(x.dtype), window_strides=(1,),
        padding=[(pad, taps - 1 - pad)], dimension_numbers=('NWC', 'WIO', 'NWC'),
        feature_group_count=x.shape[-1])
    return y + b.astype(x.dtype)


def axial_rope_angles(rows):
    row = jnp.repeat(jnp.arange(rows, dtype=jnp.float32), GRID_W)
    col = jnp.tile(jnp.arange(GRID_W, dtype=jnp.float32), rows)
    axis_dim = QK_ROPE_DIM // 2
    inv = 1.0 / (ROPE_BASE ** (jnp.arange(0, axis_dim, 2, dtype=jnp.float32) / axis_dim))
    ang = jnp.concatenate([row[:, None] * inv, col[:, None] * inv], axis=-1)
    return jnp.cos(ang), jnp.sin(ang)


def apply_rope(x, cos, sin):
    half = x.shape[-1] // 2
    shape = (cos.shape[0],) + (1,) * (x.ndim - 3) + (half,)
    cs, sn = cos.reshape(shape), sin.reshape(shape)
    xf = x.astype(jnp.float32)
    x1, x2 = xf[..., :half], xf[..., half:]
    return jnp.concatenate([x1 * cs - x2 * sn, x1 * sn + x2 * cs], axis=-1).astype(x.dtype)


def conformer_conv(u, dw_w, dw_b, ln_g, ln_b):
    a, gate = jnp.split(u, 2, axis=-1)
    y = a * jax.nn.sigmoid(gate)
    y = depthwise_conv(y, dw_w, dw_b)
    return jax.nn.silu(layer_norm(y, ln_g, ln_b))


def mla_queries(c_q, q_norm_g, w_uq, cos, sin):
    b, n, _ = c_q.shape
    q = (rms_norm(c_q, q_norm_g) @ w_uq).reshape(b, n, MLA_HEADS, QK_NOPE_DIM + QK_ROPE_DIM)
    q_nope, q_rope = q[..., :QK_NOPE_DIM], q[..., QK_NOPE_DIM:]
    if cos is not None:
        q_rope = apply_rope(q_rope, cos, sin)
    return jnp.concatenate([q_nope, q_rope], axis=-1)


def mla_keys_values(c_kv, k_rope, kv_norm_g, w_ukv, cos, sin):
    b, n, _ = c_kv.shape
    kv = (rms_norm(c_kv, kv_norm_g) @ w_ukv).reshape(b, n, MLA_HEADS, QK_NOPE_DIM + V_HEAD_DIM)
    k_nope, v = kv[..., :QK_NOPE_DIM], kv[..., QK_NOPE_DIM:]
    if cos is not None:
        k_rope = apply_rope(k_rope, cos, sin)
    k_rope = jnp.broadcast_to(k_rope[:, :, None, :], (b, n, MLA_HEADS, QK_ROPE_DIM))
    return jnp.concatenate([k_nope, k_rope], axis=-1), v


def attend(q, k, v):
    b, lq, h, dqk = q.shape
    scale = (QK_NOPE_DIM + QK_ROPE_DIM) ** -0.5

    def block(qb):
        s = jnp.einsum('bqhd,bkhd->bhqk', qb, k, preferred_element_type=jnp.float32) * scale
        p = jax.nn.softmax(s, axis=-1).astype(v.dtype)
        return jnp.einsum('bhqk,bkhd->bqhd', p, v)

    nblk = lq // ATTN_BLOCK
    qb = q.reshape(b, nblk, ATTN_BLOCK, h, dqk).transpose(1, 0, 2, 3, 4)
    out = lax.map(block, qb)
    return out.transpose(1, 0, 2, 3, 4).reshape(b, lq, h * v.shape[-1])


def hyena_filters_freq(n, w1, b1, freq1, w2, b2, freq2, w3):
    pos = jnp.arange(n, dtype=jnp.float32)[:, None]
    t = jnp.linspace(0.0, 1.0, n, dtype=jnp.float32)[:, None]
    w = 2.0 * math.pi * pos / n
    f = jnp.linspace(1e-4, HYENA_BANDS - 1, HYENA_BANDS, dtype=jnp.float32)[None, :]
    emb = jnp.concatenate([t, jnp.cos(f * w), -jnp.sin(f * w)], axis=-1)
    hid = jnp.sin(freq1 * (emb @ w1 + b1))
    hid = jnp.sin(freq2 * (hid @ w2 + b2))
    filt = (hid @ w3).astype(jnp.float32).reshape(n, 2, HYENA_ORDER, HYENA_WIDTH)
    min_decay = math.log(HYENA_DECAY_TARGET) / HYENA_SLOW_DECAY_PCT
    max_decay = math.log(HYENA_DECAY_TARGET) / HYENA_FAST_DECAY_PCT
    deltas = jnp.linspace(min_decay, max_decay, HYENA_WIDTH, dtype=jnp.float32)
    filt = filt * jnp.exp(-t[:, :, None, None] * jnp.abs(deltas))
    full = jnp.concatenate([filt[:, 0], jnp.zeros((1, HYENA_ORDER, HYENA_WIDTH), jnp.float32), filt[:0:-1, 1]], axis=0)
    full = full / jnp.sum(jnp.abs(full), axis=0, keepdims=True)
    return jnp.fft.rfft(full, axis=0)


def hyena_mix(u, short_w, short_b, w1, b1, freq1, w2, b2, freq2, w3, bias):
    n = u.shape[1]
    u = depthwise_conv(u, short_w, short_b)
    x1, x2, v = jnp.split(u.astype(jnp.float32), 3, axis=-1)
    k_f = hyena_filters_freq(n, w1, b1, freq1, w2, b2, freq2, w3)
    z = v
    for o, gate in enumerate((x1, x2)):
        z_f = jnp.fft.rfft(z, n=2 * n, axis=1)
        y = jnp.fft.irfft(z_f * k_f[:, o], n=2 * n, axis=1)[:, :n]
        z = gate * (y + z * bias[o].astype(jnp.float32))
    return z.astype(u.dtype)


def setup_inputs(seed: int = 0) -> dict:
    key = jax.random.key(seed)
    counter = [0]

    def nrm(shape, scale=1.0):
        counter[0] += 1
        return jax.random.normal(jax.random.fold_in(key, counter[0]), shape, jnp.float32) * scale

    def gain(shape):
        return 1.0 + 0.05 * nrm(shape)

    L = DEPTH
    D = D_MODEL
    return {
        'x': nrm((BATCH, SEQ, D)),
        'c': nrm((BATCH, D)),
        'ctx': nrm((BATCH, CTX_LEN, D)),
        'c_ctx': nrm((D,)),
        'w_mod': nrm((L, D, N_MOD * D), 0.5 * D ** -0.5),
        'b_mod': nrm((L, N_MOD * D), 0.02),
        'g_pre_mix': gain((L, D)),
        'g_post_mix': gain((L, D)),
        'g_pre_ffn': gain((L, D)),
        'g_post_ffn': gain((L, D)),
        'w_in': nrm((L, D, IN_WIDTH), D ** -0.5),
        'conv_dw_w': nrm((L, CONV_TAPS, CONV_WIDTH), CONV_TAPS ** -0.5),
        'conv_dw_b': nrm((L, CONV_WIDTH), 0.02),
        'conv_ln_g': gain((L, CONV_WIDTH)),
        'conv_ln_b': nrm((L, CONV_WIDTH), 0.02),
        'mla_q_norm': gain((L, Q_LORA_RANK)),
        'mla_w_uq': nrm((L, Q_LORA_RANK, MLA_HEADS * (QK_NOPE_DIM + QK_ROPE_DIM)), Q_LORA_RANK ** -0.5),
        'mla_kv_norm': gain((L, KV_LORA_RANK)),
        'mla_w_ukv': nrm((L, KV_LORA_RANK, MLA_HEADS * (QK_NOPE_DIM + V_HEAD_DIM)), KV_LORA_RANK ** -0.5),
        'hy_short_w': nrm((L, HYENA_SHORT_TAPS, IN_C), HYENA_SHORT_TAPS ** -0.5),
        'hy_short_b': nrm((L, IN_C), 0.02),
        'hy_w1': nrm((L, HYENA_EMB_DIM, HYENA_FILTER_HIDDEN), HYENA_EMB_DIM ** -0.5),
        'hy_b1': nrm((L, HYENA_FILTER_HIDDEN), 0.02),
        'hy_freq1': gain((L, HYENA_FILTER_HIDDEN)),
        'hy_w2': nrm((L, HYENA_FILTER_HIDDEN, HYENA_FILTER_HIDDEN), HYENA_FILTER_HIDDEN ** -0.5),
        'hy_b2': nrm((L, HYENA_FILTER_HIDDEN), 0.02),
        'hy_freq2': gain((L, HYENA_FILTER_HIDDEN)),
        'hy_w3': nrm((L, HYENA_FILTER_HIDDEN, 2 * HYENA_ORDER * HYENA_WIDTH), HYENA_FILTER_HIDDEN ** -0.5),
        'hy_bias': nrm((L, HYENA_ORDER, HYENA_WIDTH)),
        'w_out': nrm((L, MIX_WIDTH, D), MIX_WIDTH ** -0.5),
        'w_ff1': nrm((L, D, D_FF), D ** -0.5),
        'w_ff2': nrm((L, D_FF, D), D_FF ** -0.5),
    }


def reference(x, c, ctx, c_ctx, w_mod, b_mod, g_pre_mix, g_post_mix, g_pre_ffn, g_post_ffn, w_in,
              conv_dw_w, conv_dw_b, conv_ln_g, conv_ln_b, mla_q_norm, mla_w_uq, mla_kv_norm, mla_w_ukv,
              hy_short_w, hy_short_b, hy_w1, hy_b1, hy_freq1, hy_w2, hy_b2, hy_freq2, hy_w3, hy_bias,
              w_out, w_ff1, w_ff2):
    ROWS = x.shape[1] // GRID_W
    cos, sin = axial_rope_angles(ROWS)
    xc = ctx
    silu_c = jax.nn.silu(c)[:, None, :]
    silu_cc = jax.nn.silu(c_ctx)[None, None, :]
    qa, qb_end = Q_LORA_RANK, Q_LORA_RANK + KV_LORA_RANK

    for l in range(DEPTH):
        last = l == DEPTH - 1
        mx = jnp.split(silu_c @ w_mod[l] + b_mod[l], N_MOD, axis=-1)
        mc = jnp.split(silu_cc @ w_mod[l] + b_mod[l], N_MOD, axis=-1)

        def token_mixer(p_a, attn, p_c):
            conv_o = conformer_conv(p_a, conv_dw_w[l], conv_dw_b[l], conv_ln_g[l], conv_ln_b[l])
            hy_o = hyena_mix(p_c, hy_short_w[l], hy_short_b[l], hy_w1[l], hy_b1[l], hy_freq1[l],
                             hy_w2[l], hy_b2[l], hy_freq2[l], hy_w3[l], hy_bias[l])
            return jnp.concatenate([conv_o, attn, hy_o], axis=-1) @ w_out[l]

        def channel_mixer(stream, m):
            h2 = modulate(rms_norm(stream, g_pre_ffn[l]), m[3], m[4])
            y = jnp.square(jax.nn.relu(h2 @ w_ff1[l])) @ w_ff2[l]
            return stream + m[5] * rms_norm(y, g_post_ffn[l])

        h = modulate(rms_norm(x, g_pre_mix[l]), mx[0], mx[1])
        hc = modulate(rms_norm(xc, g_pre_mix[l]), mc[0], mc[1])

        pc_b = hc @ w_in[l][:, IN_A:IN_A + IN_B]
        kc, vc = mla_keys_values(pc_b[..., qa:qb_end], pc_b[..., qb_end:], mla_kv_norm[l], mla_w_ukv[l], None, None)

        p = h @ w_in[l]
        p_a, p_b, p_c = p[..., :IN_A], p[..., IN_A:IN_A + IN_B], p[..., IN_A + IN_B:]
        q = mla_queries(p_b[..., :qa], mla_q_norm[l], mla_w_uq[l], cos, sin)
        k, v = mla_keys_values(p_b[..., qa:qb_end], p_b[..., qb_end:], mla_kv_norm[l], mla_w_ukv[l], cos, sin)
        attn = attend(q, jnp.concatenate([kc, k], axis=1), jnp.concatenate([vc, v], axis=1))
        new_x = x + mx[2] * rms_norm(token_mixer(p_a, attn, p_c), g_post_mix[l])
        new_x = channel_mixer(new_x, mx)

        if not last:
            pc_a = hc @ w_in[l][:, :IN_A]
            pc_c = hc @ w_in[l][:, IN_A + IN_B:]
            qc = mla_queries(pc_b[..., :qa], mla_q_norm[l], mla_w_uq[l], None, None)
            attn_c = attend(qc, kc, vc)
            xc = xc + mc[2] * rms_norm(token_mixer(pc_a, attn_c, pc_c), g_post_mix[l])
            xc = channel_mixer(xc, mc)
        x = new_x
    return x
```

```python
import functools
import math

import numpy as np
import jax
import jax.numpy as jnp
from jax import lax
from jax.experimental import pallas as pl
from jax.experimental.pallas import tpu as pltpu

GRID_W = 64
CONV_WIDTH = 512
CONV_TAPS = 31
MLA_HEADS = 8
QK_NOPE_DIM = 128
QK_ROPE_DIM = 64
V_HEAD_DIM = 128
Q_LORA_RANK = 512
KV_LORA_RANK = 256
MLA_WIDTH = MLA_HEADS * V_HEAD_DIM
HYENA_WIDTH = 512
HYENA_ORDER = 2
HYENA_SHORT_TAPS = 3
HYENA_EMB_DIM = 33
HYENA_BANDS = (HYENA_EMB_DIM - 1) // 2
HYENA_FILTER_HIDDEN = 64
HYENA_FAST_DECAY_PCT = 0.3
HYENA_SLOW_DECAY_PCT = 1.5
HYENA_DECAY_TARGET = 1e-2
N_MOD = 6
ROPE_BASE = 10000.0
NORM_EPS = 1e-6
IN_A = 2 * CONV_WIDTH
IN_B = Q_LORA_RANK + KV_LORA_RANK + QK_ROPE_DIM
IN_C = (HYENA_ORDER + 1) * HYENA_WIDTH
QK_DIM = QK_NOPE_DIM + QK_ROPE_DIM

V7X_LANES = 128
V7X_SUBLANES = 8
V7X_VMEM_BYTES = 64 * 1024 * 1024
V7X_VMEM_BUDGET = 56 * 1024 * 1024

FFT_N2 = 256

F32 = jnp.float32
BF16 = jnp.bfloat16
HIGHEST = lax.Precision.HIGHEST


def _cparams(semantics, vmem_bytes):
    return pltpu.CompilerParams(dimension_semantics=semantics,
                                vmem_limit_bytes=int(min(max(vmem_bytes, 16 << 20), V7X_VMEM_BUDGET)))


def _resident(block_shape, index_map):
    return pl.BlockSpec(block_shape, index_map, pipeline_mode=pl.Buffered(1))


def _rms(x, g):
    ms = jnp.mean(x * x, axis=-1, keepdims=True)
    return x * lax.rsqrt(ms + NORM_EPS) * g


def _silu(x):
    return x * jax.nn.sigmoid(x)


def _mod_kernel(cv_ref, w_ref, b_ref, o_ref):
    s = _silu(cv_ref[...])
    o_ref[...] = jnp.dot(s, w_ref[...], preferred_element_type=F32, precision=HIGHEST) + b_ref[...]


def modulation(cvec, w_mod, b_mod):
    n_layers, d, width = w_mod.shape
    tn = 1024
    return pl.pallas_call(
        _mod_kernel,
        out_shape=jax.ShapeDtypeStruct((n_layers, V7X_SUBLANES, width), F32),
        grid=(n_layers, width // tn),
        in_specs=[pl.BlockSpec((V7X_SUBLANES, d), lambda l, j: (0, 0)),
                  pl.BlockSpec((None, d, tn), lambda l, j: (l, 0, j)),
                  pl.BlockSpec((None, 1, tn), lambda l, j: (l, 0, j))],
        out_specs=pl.BlockSpec((None, V7X_SUBLANES, tn), lambda l, j: (l, 0, j)),
        compiler_params=_cparams(("parallel", "parallel"), 2 * d * tn * 4 + (4 << 20)),
        name="modulation",
    )(cvec, w_mod, b_mod.reshape(n_layers, 1, width))


_IN_SPLITS = (IN_A, IN_C, Q_LORA_RANK, KV_LORA_RANK, 2 * QK_ROPE_DIM)
_IN_WIDTH = sum(_IN_SPLITS)


def _in_kernel(x_ref, g_ref, sh_ref, sc_ref, w_ref, *o_refs):
    h = _rms(x_ref[...], g_ref[...]) * (1.0 + sc_ref[...]) + sh_ref[...]
    hb = h.astype(BF16)
    off = 0
    for o_ref, width in zip(o_refs, _IN_SPLITS):
        o_ref[...] = jnp.dot(hb, w_ref[:, off:off + width], preferred_element_type=F32)
        off += width


def in_projection(x, g, shift, scale, w_in_bf16):
    n, d = x.shape
    tm = min(512, n)
    vec = pl.BlockSpec((1, d), lambda i: (0, 0))
    vmem = 2 * tm * d * 4 + d * _IN_WIDTH * 2 + 3 * tm * _IN_WIDTH * 4 + (4 << 20)
    return pl.pallas_call(
        _in_kernel,
        out_shape=[jax.ShapeDtypeStruct((n, w), F32) for w in _IN_SPLITS],
        grid=(n // tm,),
        in_specs=[pl.BlockSpec((tm, d), lambda i: (i, 0)), vec, vec, vec,
                  _resident((d, _IN_WIDTH), lambda i: (0, 0))],
        out_specs=[pl.BlockSpec((tm, w), lambda i: (i, 0)) for w in _IN_SPLITS],
        compiler_params=_cparams(("parallel",), vmem),
        name="in_projection",
    )(x, g, shift, scale, w_in_bf16)


_CONV_HALO = 16
_CONV_ROWS = 32


def _conv_kernel(prev_ref, cur_ref, next_ref, w_ref, b_ref, lg_ref, lb_ref, o_ref, ybuf, *, tile):
    i = pl.program_id(0)
    last = pl.num_programs(0) - 1

    def glu(v):
        return v[:, :CONV_WIDTH] * jax.nn.sigmoid(v[:, CONV_WIDTH:])

    ybuf[0:_CONV_HALO, :] = jnp.where(i > 0, glu(prev_ref[...]), 0.0)
    ybuf[_CONV_HALO:_CONV_HALO + tile, :] = glu(cur_ref[...])
    ybuf[_CONV_HALO + tile:2 * _CONV_HALO + tile, :] = jnp.where(i < last, glu(next_ref[...]), 0.0)

    pad = (CONV_TAPS - 1) // 2
    w = w_ref[...]

    def chunk(r, carry):
        r0 = pl.multiple_of(r * _CONV_ROWS, _CONV_ROWS)
        win = ybuf[pl.ds(r0, _CONV_ROWS + 2 * _CONV_HALO), :]
        acc = jnp.zeros((_CONV_ROWS, CONV_WIDTH), F32)
        for k in range(CONV_TAPS):
            off = _CONV_HALO - pad + k
            acc = acc + win[off:off + _CONV_ROWS, :] * w[k:k + 1, :]
        y = acc + b_ref[...]
        mu = jnp.mean(y, axis=-1, keepdims=True)
        yc = y - mu
        var = jnp.mean(yc * yc, axis=-1, keepdims=True)
        z = yc * lax.rsqrt(var + NORM_EPS) * lg_ref[...] + lb_ref[...]
        o_ref[pl.ds(r0, _CONV_ROWS), :] = _silu(z).astype(o_ref.dtype)
        return carry

    lax.fori_loop(0, tile // _CONV_ROWS, chunk, 0)


def conformer_conv(p_a, dw_w, dw_b, ln_g, ln_b):
    n = p_a.shape[0]
    tile = min(512, n)
    hb = tile // _CONV_HALO
    nhb = n // _CONV_HALO
    vec = pl.BlockSpec((1, CONV_WIDTH), lambda i: (0, 0))
    return pl.pallas_call(
        functools.partial(_conv_kernel, tile=tile),
        out_shape=jax.ShapeDtypeStruct((n, CONV_WIDTH), BF16),
        grid=(n // tile,),
        in_specs=[pl.BlockSpec((_CONV_HALO, IN_A), lambda i: (jnp.maximum(i * hb - 1, 0), 0)),
                  pl.BlockSpec((tile, IN_A), lambda i: (i, 0)),
                  pl.BlockSpec((_CONV_HALO, IN_A), lambda i: (jnp.minimum((i + 1) * hb, nhb - 1), 0)),
                  pl.BlockSpec((CONV_TAPS, CONV_WIDTH), lambda i: (0, 0)), vec, vec, vec],
        out_specs=pl.BlockSpec((tile, CONV_WIDTH), lambda i: (i, 0)),
        scratch_shapes=[pltpu.VMEM((tile + 2 * _CONV_HALO, CONV_WIDTH), F32)],
        compiler_params=_cparams(("parallel",), 4 * tile * IN_A * 4 + (8 << 20)),
        name="conformer_conv",
    )(p_a, p_a, p_a, dw_w, dw_b, ln_g, ln_b)


def _rope_halves(t):
    return t[:, :QK_ROPE_DIM] + t[:, QK_ROPE_DIM:]


def _qkv_kernel(cq_ref, ckv_ref, kr_ref, cs_ref, gq_ref, gkv_ref, wq_ref, wkv_ref, q_ref, k_ref, v_ref):
    cqn = _rms(cq_ref[...], gq_ref[...]).astype(BF16)
    ckvn = _rms(ckv_ref[...], gkv_ref[...]).astype(BF16)
    cs = cs_ref[...]
    k_rope = _rope_halves(kr_ref[...] * cs).astype(k_ref.dtype)
    for h in range(MLA_HEADS):
        qa = jnp.dot(cqn, wq_ref[h], preferred_element_type=F32)
        q_ref[h, :, 0:QK_NOPE_DIM] = qa[:, :QK_NOPE_DIM].astype(q_ref.dtype)
        q_ref[h, :, QK_NOPE_DIM:QK_DIM] = _rope_halves(qa[:, QK_NOPE_DIM:] * cs).astype(q_ref.dtype)
        kva = jnp.dot(ckvn, wkv_ref[h], preferred_element_type=F32)
        k_ref[h, :, 0:QK_NOPE_DIM] = kva[:, :QK_NOPE_DIM].astype(k_ref.dtype)
        k_ref[h, :, QK_NOPE_DIM:QK_DIM] = k_rope
        v_ref[h] = kva[:, QK_NOPE_DIM:].astype(v_ref.dtype)


def mla_qkv(c_q, c_kv, kr, cs, g_q, g_kv, wq, wkv):
    n = c_q.shape[0]
    tm = min(512, n)
    hq = QK_DIM + QK_ROPE_DIM
    hkv = QK_NOPE_DIM + V_HEAD_DIM
    row = lambda w: pl.BlockSpec((tm, w), lambda i: (i, 0))
    return pl.pallas_call(
        _qkv_kernel,
        out_shape=[jax.ShapeDtypeStruct((MLA_HEADS, n, QK_DIM), BF16),
                   jax.ShapeDtypeStruct((MLA_HEADS, n, QK_DIM), BF16),
                   jax.ShapeDtypeStruct((MLA_HEADS, n, V_HEAD_DIM), BF16)],
        grid=(n // tm,),
        in_specs=[row(Q_LORA_RANK), row(KV_LORA_RANK), row(2 * QK_ROPE_DIM), row(2 * QK_ROPE_DIM),
                  pl.BlockSpec((1, Q_LORA_RANK), lambda i: (0, 0)),
                  pl.BlockSpec((1, KV_LORA_RANK), lambda i: (0, 0)),
                  pl.BlockSpec((MLA_HEADS, Q_LORA_RANK, hq), lambda i: (0, 0, 0)),
                  pl.BlockSpec((MLA_HEADS, KV_LORA_RANK, hkv), lambda i: (0, 0, 0))],
        out_specs=[pl.BlockSpec((MLA_HEADS, tm, QK_DIM), lambda i: (0, i, 0)),
                   pl.BlockSpec((MLA_HEADS, tm, QK_DIM), lambda i: (0, i, 0)),
                   pl.BlockSpec((MLA_HEADS, tm, V_HEAD_DIM), lambda i: (0, i, 0))],
        compiler_params=_cparams(("parallel",), 32 << 20),
        name="mla_qkv",
    )(c_q, c_kv, kr, cs, g_q, g_kv, wq, wkv)


def _attn_kernel(*refs, chunks):
    n_src = len(chunks)
    q_ref = refs[0]
    kv_refs = refs[1:1 + 2 * n_src]
    o_ref = refs[1 + 2 * n_src]
    m_sc, l_sc, acc_sc = refs[2 + 2 * n_src:]
    q = q_ref[...]
    c = (QK_DIM ** -0.5) * math.log2(math.e)
    m_sc[...] = jnp.full(m_sc.shape, -jnp.inf, F32)
    l_sc[...] = jnp.zeros(l_sc.shape, F32)
    acc_sc[...] = jnp.zeros(acc_sc.shape, F32)

    def step(kb, vb):
        s = lax.dot_general(q, kb, (((1,), (1,)), ((), ())), preferred_element_type=F32)
        m_prev = m_sc[...]
        m_new = jnp.maximum(m_prev, jnp.max(s, axis=-1, keepdims=True))
        alpha = jnp.exp2((m_prev - m_new) * c)
        p = jnp.exp2((s - m_new) * c)
        l_sc[...] = alpha * l_sc[...] + jnp.sum(p, axis=-1, keepdims=True)
        acc_sc[...] = alpha * acc_sc[...] + jnp.dot(p.astype(vb.dtype), vb, preferred_element_type=F32)
        m_sc[...] = m_new

    for s_idx, (length, tk) in enumerate(chunks):
        k_ref, v_ref = kv_refs[2 * s_idx], kv_refs[2 * s_idx + 1]
        if length == tk:
            step(k_ref[...], v_ref[...])
        else:
            def body(j, carry, k_ref=k_ref, v_ref=v_ref, tk=tk):
                r0 = pl.multiple_of(j * tk, tk)
                step(k_ref[pl.ds(r0, tk), :], v_ref[pl.ds(r0, tk), :])
                return carry
            lax.fori_loop(0, length // tk, body, 0)

    o_ref[...] = (acc_sc[...] / l_sc[...]).astype(o_ref.dtype)


def attention(q, kv_sources):
    heads, lq, _ = q.shape
    tq = min(512, lq)
    chunks = []
    in_specs = [pl.BlockSpec((None, tq, QK_DIM), lambda h, i: (h, i, 0))]
    operands = [q]
    kv_bytes = 0
    for k, v in kv_sources:
        lk = k.shape[1]
        chunks.append((lk, min(512, lk)))
        in_specs.append(pl.BlockSpec((None, lk, QK_DIM), lambda h, i: (h, 0, 0)))
        in_specs.append(pl.BlockSpec((None, lk, V_HEAD_DIM), lambda h, i: (h, 0, 0)))
        operands += [k, v]
        kv_bytes += lk * (2 * V7X_LANES + V_HEAD_DIM) * 2
    return pl.pallas_call(
        functools.partial(_attn_kernel, chunks=tuple(chunks)),
        out_shape=jax.ShapeDtypeStruct((lq, heads * V_HEAD_DIM), BF16),
        grid=(heads, lq // tq),
        in_specs=in_specs,
        out_specs=pl.BlockSpec((tq, V_HEAD_DIM), lambda h, i: (i, h)),
        scratch_shapes=[pltpu.VMEM((tq, 1), F32), pltpu.VMEM((tq, 1), F32), pltpu.VMEM((tq, V_HEAD_DIM), F32)],
        compiler_params=_cparams(("parallel", "arbitrary"), 2 * kv_bytes + (16 << 20)),
        name="attention",
    )(*operands)


_SHORT_HALO = 8


def _short_kernel(prev_ref, cur_ref, next_ref, w_ref, b_ref, x1_ref, x2_ref, v_ref, ubuf, *, tile):
    i = pl.program_id(0)
    last = pl.num_programs(0) - 1
    ubuf[0:_SHORT_HALO, :] = jnp.where(i > 0, prev_ref[...], 0.0)
    ubuf[_SHORT_HALO:_SHORT_HALO + tile, :] = cur_ref[...]
    ubuf[_SHORT_HALO + tile:2 * _SHORT_HALO + tile, :] = jnp.where(i < last, next_ref[...], 0.0)
    pad = (HYENA_SHORT_TAPS - 1) // 2
    w = w_ref[...]
    for o_ref, c0 in ((x1_ref, 0), (x2_ref, HYENA_WIDTH), (v_ref, 2 * HYENA_WIDTH)):
        acc = jnp.zeros((tile, HYENA_WIDTH), F32) + b_ref[:, c0:c0 + HYENA_WIDTH]
        for k in range(HYENA_SHORT_TAPS):
            acc = acc + (ubuf[_SHORT_HALO - pad + k:_SHORT_HALO - pad + k + tile, c0:c0 + HYENA_WIDTH]
                         * w[k:k + 1, c0:c0 + HYENA_WIDTH])
        o_ref[...] = acc


def hyena_short_conv(p_c, w, b):
    n = p_c.shape[0]
    tile = min(256, n)
    hb = tile // _SHORT_HALO
    nhb = n // _SHORT_HALO
    out = jax.ShapeDtypeStruct((n, HYENA_WIDTH), F32)
    o_spec = pl.BlockSpec((tile, HYENA_WIDTH), lambda i: (i, 0))
    return pl.pallas_call(
        functools.partial(_short_kernel, tile=tile),
        out_shape=[out, out, out],
        grid=(n // tile,),
        in_specs=[pl.BlockSpec((_SHORT_HALO, IN_C), lambda i: (jnp.maximum(i * hb - 1, 0), 0)),
                  pl.BlockSpec((tile, IN_C), lambda i: (i, 0)),
                  pl.BlockSpec((_SHORT_HALO, IN_C), lambda i: (jnp.minimum((i + 1) * hb, nhb - 1), 0)),
                  pl.BlockSpec((HYENA_SHORT_TAPS, IN_C), lambda i: (0, 0)),
                  pl.BlockSpec((1, IN_C), lambda i: (0, 0))],
        out_specs=[o_spec, o_spec, o_spec],
        scratch_shapes=[pltpu.VMEM((tile + 2 * _SHORT_HALO, IN_C), F32)],
        compiler_params=_cparams(("parallel",), 32 << 20),
        name="hyena_short_conv",
    )(p_c, p_c, p_c, w, b)


_FILT_W = HYENA_ORDER * HYENA_WIDTH
_EMB_PAD = 128


def _filter_kernel(emb_ref, w1_ref, b1_ref, f1_ref, w2_ref, b2_ref, f2_ref, w3_ref, dl_ref,
                   full_ref, l1_ref, *, tiles_per_dir):
    i = pl.program_id(0)
    emb = emb_ref[...]
    hid = jnp.sin(f1_ref[...] * (jnp.dot(emb, w1_ref[...], preferred_element_type=F32, precision=HIGHEST)
                                 + b1_ref[...]))
    hid = jnp.sin(f2_ref[...] * (jnp.dot(hid, w2_ref[...], preferred_element_type=F32, precision=HIGHEST)
                                 + b2_ref[...]))
    filt = jnp.dot(hid, w3_ref[...], preferred_element_type=F32, precision=HIGHEST)
    decay = jnp.exp(-emb[:, 0:1] * dl_ref[...])
    rows = lax.broadcasted_iota(jnp.int32, (filt.shape[0], 1), 0)
    keep = jnp.logical_or(i != tiles_per_dir, rows != 0)
    for o in range(HYENA_ORDER):
        sl = slice(o * HYENA_WIDTH, (o + 1) * HYENA_WIDTH)
        full_ref[:, sl] = jnp.where(keep, filt[:, sl] * decay, 0.0)

    @pl.when(i == 0)
    def _():
        l1_ref[...] = jnp.zeros(l1_ref.shape, F32)

    l1_ref[0:1, :] += jnp.sum(jnp.abs(full_ref[...]), axis=0, keepdims=True)


def hyena_filters(n, w1, b1, f1, w2, b2, f2, w3):
    pos = jnp.arange(n + 1, dtype=F32)[:, None]
    t = jnp.concatenate([jnp.linspace(0.0, 1.0, n, dtype=F32), jnp.ones((1,), F32)])[:, None]
    wpos = 2.0 * math.pi * pos / n
    f = jnp.linspace(1e-4, HYENA_BANDS - 1, HYENA_BANDS, dtype=F32)[None, :]
    emb = jnp.concatenate([t, jnp.cos(f * wpos), -jnp.sin(f * wpos)], axis=-1)
    emb_full = jnp.concatenate([emb[:n], emb[n:0:-1]], axis=0)
    emb_full = jnp.pad(emb_full, ((0, 0), (0, _EMB_PAD - HYENA_EMB_DIM)))
    w1p = jnp.pad(w1, ((0, _EMB_PAD - HYENA_EMB_DIM), (0, 0)))
    w3d = w3.reshape(HYENA_FILTER_HIDDEN, 2, _FILT_W).transpose(1, 0, 2)
    min_decay = math.log(HYENA_DECAY_TARGET) / HYENA_SLOW_DECAY_PCT
    max_decay = math.log(HYENA_DECAY_TARGET) / HYENA_FAST_DECAY_PCT
    deltas = jnp.abs(jnp.linspace(min_decay, max_decay, HYENA_WIDTH, dtype=F32))[None, :]
    tile = min(512, n)
    tiles_per_dir = n // tile
    hv = pl.BlockSpec((1, HYENA_FILTER_HIDDEN), lambda i: (0, 0))
    return pl.pallas_call(
        functools.partial(_filter_kernel, tiles_per_dir=tiles_per_dir),
        out_shape=[jax.ShapeDtypeStruct((2 * n, _FILT_W), F32),
                   jax.ShapeDtypeStruct((V7X_SUBLANES, _FILT_W), F32)],
        grid=(2 * tiles_per_dir,),
        in_specs=[pl.BlockSpec((tile, _EMB_PAD), lambda i: (i, 0)),
                  pl.BlockSpec((_EMB_PAD, HYENA_FILTER_HIDDEN), lambda i: (0, 0)), hv, hv,
                  pl.BlockSpec((HYENA_FILTER_HIDDEN, HYENA_FILTER_HIDDEN), lambda i: (0, 0)), hv, hv,
                  pl.BlockSpec((None, HYENA_FILTER_HIDDEN, _FILT_W), lambda i: (i // tiles_per_dir, 0, 0)),
                  pl.BlockSpec((1, HYENA_WIDTH), lambda i: (0, 0))],
        out_specs=[pl.BlockSpec((tile, _FILT_W), lambda i: (i, 0)),
                   pl.BlockSpec((V7X_SUBLANES, _FILT_W), lambda i: (0, 0))],
        compiler_params=_cparams(("arbitrary",), 32 << 20),
        name="hyena_filters",
    )(emb_full, w1p, b1, f1, w2, b2, f2, w3d, deltas)


def _fft_tables(n1):
    n2 = FFT_N2
    nb = n1 // 2 + 1
    nbp = -(-nb // V7X_SUBLANES) * V7X_SUBLANES
    k1 = np.arange(nbp)[:, None].astype(np.float64)
    valid = (np.arange(nbp) < nb)[:, None]
    s1 = np.arange(n1)[None, :]
    ang = 2.0 * np.pi * k1 * s1 / n1
    f1 = np.concatenate([np.where(valid, np.cos(ang), 0.0), np.where(valid, -np.sin(ang), 0.0)], axis=0)
    idx = np.arange(n2)
    ang2 = 2.0 * np.pi * np.outer(idx, idx) / n2
    f2 = np.stack([np.cos(ang2), -np.sin(ang2)])
    angt = 2.0 * np.pi * k1 * idx[None, :] / (n1 * n2)
    tw_f = np.stack([np.cos(angt), -np.sin(angt)])[:, :, None, :]
    t1 = np.arange(n1 // 2)[:, None]
    ang1 = 2.0 * np.pi * t1 * np.arange(nbp)[None, :] / n1
    inv1 = np.stack([np.cos(ang1), np.sin(ang1)])
    wk = np.where((np.arange(nbp) == 0) | (np.arange(nbp) == n1 // 2), 1.0, 2.0) * (np.arange(nbp) < nb)
    angb = 2.0 * np.pi * idx[:, None] * np.arange(nbp)[None, :] / (n1 * n2)
    tw_i = np.stack([np.cos(angb) * wk, np.sin(angb) * wk], axis=1) / (n1 * n2)
    as32 = lambda a: jnp.asarray(a, dtype=F32)
    return dict(nb=nb, nbp=nbp, f1=as32(f1), f2=as32(f2), tw_f=as32(tw_f), inv1=as32(inv1), tw_i=as32(tw_i))


def _fft1_kernel(x_ref, f_ref, o_ref):
    nbp = o_ref.shape[1]
    a = jnp.dot(f_ref[...], x_ref[...], preferred_element_type=F32, precision=HIGHEST)
    o_ref[0] = a[:nbp]
    o_ref[1] = a[nbp:]


def fft_level1(x, f1, n1):
    rows, ch = x.shape
    s1 = rows // FFT_N2
    cols = FFT_N2 * ch
    nbp = f1.shape[0] // 2
    tc = min(8192, cols)
    out = pl.pallas_call(
        _fft1_kernel,
        out_shape=jax.ShapeDtypeStruct((2, nbp, cols), F32),
        grid=(cols // tc,),
        in_specs=[pl.BlockSpec((s1, tc), lambda j: (0, j)),
                  pl.BlockSpec((2 * nbp, s1), lambda j: (0, 0))],
        out_specs=pl.BlockSpec((2, nbp, tc), lambda j: (0, 0, j)),
        compiler_params=_cparams(("parallel",), 2 * (s1 + 4 * nbp) * tc * 4 + (8 << 20)),
        name="fft_level1",
    )(x.reshape(s1, cols), f1[:, :s1])
    return out.reshape(2, nbp, FFT_N2, ch)


def _cdot(ar, ai, br, bi):
    dot = functools.partial(jnp.dot, preferred_element_type=F32, precision=HIGHEST)
    return dot(ar, br) - dot(ai, bi), dot(ar, bi) + dot(ai, br)


def _twiddled_dft(f_ref, t_ref):
    fr, fi = f_ref[0], f_ref[1]
    tr, ti = t_ref[0], t_ref[1]
    return fr * tr - fi * ti, fr * ti + fi * tr


def _fft2_spectrum_kernel(a_ref, f_ref, t_ref, l1_ref, o_ref, *, nb):
    k1 = pl.program_id(0)

    @pl.when(k1 < nb)
    def _():
        gr, gi = _twiddled_dft(f_ref, t_ref)
        xr, xi = _cdot(gr, gi, a_ref[0], a_ref[1])
        inv = 1.0 / l1_ref[0:1, :]
        o_ref[0] = xr * inv
        o_ref[1] = xi * inv

    @pl.when(k1 >= nb)
    def _():
        o_ref[...] = jnp.zeros(o_ref.shape, F32)


def fft_level2_spectrum(a, tabs, l1):
    _, nbp, n2, ch = a.shape
    blk = pl.BlockSpec((2, None, n2, ch), lambda k: (0, k, 0, 0))
    return pl.pallas_call(
        functools.partial(_fft2_spectrum_kernel, nb=tabs["nb"]),
        out_shape=jax.ShapeDtypeStruct(a.shape, F32),
        grid=(nbp,),
        in_specs=[blk, pl.BlockSpec((2, n2, n2), lambda k: (0, 0, 0)),
                  pl.BlockSpec((2, None, 1, n2), lambda k: (0, k, 0, 0)),
                  pl.BlockSpec((V7X_SUBLANES, ch), lambda k: (0, 0))],
        out_specs=blk,
        compiler_params=_cparams(("parallel",), 8 * 2 * n2 * ch * 4 + (8 << 20)),
        name="fft_level2_spectrum",
    )(a, tabs["f2"], tabs["tw_f"], l1)


def _fft2_conv_kernel(a_ref, kf_ref, f_ref, t_ref, o_ref, *, nb):
    k1 = pl.program_id(0)

    @pl.when(k1 < nb)
    def _():
        gr, gi = _twiddled_dft(f_ref, t_ref)
        xr, xi = _cdot(gr, gi, a_ref[0], a_ref[1])
        kr, ki = kf_ref[0], kf_ref[1]
        yr = xr * kr - xi * ki
        yi = xr * ki + xi * kr
        br, bi = _cdot(f_ref[0], -f_ref[1], yr, yi)
        o_ref[0] = br
        o_ref[1] = bi

    @pl.when(k1 >= nb)
    def _():
        o_ref[...] = jnp.zeros(o_ref.shape, F32)


def fft_level2_conv(a, kf, order, tabs):
    _, nbp, n2, ch = a.shape
    blk = pl.BlockSpec((2, None, n2, ch), lambda k: (0, k, 0, 0))
    return pl.pallas_call(
        functools.partial(_fft2_conv_kernel, nb=tabs["nb"]),
        out_shape=jax.ShapeDtypeStruct(a.shape, F32),
        grid=(nbp,),
        in_specs=[blk, pl.BlockSpec((2, None, n2, ch), lambda k: (0, k, 0, order)),
                  pl.BlockSpec((2, n2, n2), lambda k: (0, 0, 0)),
                  pl.BlockSpec((2, None, 1, n2), lambda k: (0, k, 0, 0))],
        out_specs=blk,
        compiler_params=_cparams(("parallel",), 10 * 2 * n2 * ch * 4 + (8 << 20)),
        name="fft_level2_conv",
    )(a, kf, tabs["f2"], tabs["tw_f"])


_INV_T2 = 8


def _ifft1_gate_kernel(b_ref, inv_ref, tw_ref, z_ref, g_ref, bias_ref, o_ref):
    ca, sa = inv_ref[0], inv_ref[1]
    for j in range(_INV_T2):
        cb, sb = tw_ref[j, 0:1, :], tw_ref[j, 1:2, :]
        mc = ca * cb - sa * sb
        ms = -(sa * cb + ca * sb)
        y = (jnp.dot(mc, b_ref[0, :, j, :], preferred_element_type=F32, precision=HIGHEST)
             + jnp.dot(ms, b_ref[1, :, j, :], preferred_element_type=F32, precision=HIGHEST))
        z = z_ref[:, j, :]
        o_ref[:, j, :] = (g_ref[:, j, :] * (y + z * bias_ref[...])).astype(o_ref.dtype)


def ifft_level1_gate(b, z, gate, bias, tabs, out_dtype):
    _, nbp, n2, ch = b.shape
    n = z.shape[0]
    s1 = n // n2
    view = lambda a: a.reshape(s1, n2, ch)
    seq = pl.BlockSpec((s1, _INV_T2, ch), lambda j: (0, j, 0))
    out = pl.pallas_call(
        _ifft1_gate_kernel,
        out_shape=jax.ShapeDtypeStruct((s1, n2, ch), out_dtype),
        grid=(n2 // _INV_T2,),
        in_specs=[pl.BlockSpec((2, nbp, _INV_T2, ch), lambda j: (0, 0, j, 0)),
                  pl.BlockSpec((2, s1, nbp), lambda j: (0, 0, 0)),
                  pl.BlockSpec((_INV_T2, 2, nbp), lambda j: (j, 0, 0)),
                  seq, seq, pl.BlockSpec((1, ch), lambda j: (0, 0))],
        out_specs=seq,
        compiler_params=_cparams(("parallel",), 2 * (2 * nbp + 3 * s1) * _INV_T2 * ch * 4 + (8 << 20)),
        name="ifft_level1_gate",
    )(b, tabs["inv1"], tabs["tw_i"], view(z), view(gate), bias)
    return out.reshape(n, ch)


def hyena_long_conv_fft(x1, x2, v, full, l1, bias):
    n, ch = v.shape
    n1 = 2 * n // FFT_N2
    tabs = _fft_tables(n1)
    kf = fft_level2_spectrum(fft_level1(full, tabs["f1"], n1), tabs, l1)
    z = v
    for o, gate in enumerate((x1, x2)):
        a = fft_level1(z, tabs["f1"], n1)
        b = fft_level2_conv(a, kf, o, tabs)
        last = o == HYENA_ORDER - 1
        z = ifft_level1_gate(b, z, gate, bias[o:o + 1], tabs, BF16 if last else F32)
    return z


def _direct_conv_kernel(lo_ref, hi_ref, l1_ref, z_ref, g_ref, bias_ref, o_ref, ebuf, *, n):
    inv = 1.0 / l1_ref[0:1, :]
    ebuf[0:n, :] = hi_ref[...] * inv
    ebuf[n:2 * n, :] = lo_ref[...] * inv
    for c0 in range(0, HYENA_WIDTH, V7X_LANES):
        cs = slice(c0, c0 + V7X_LANES)

        def body(g, acc, cs=cs):
            s0 = pl.multiple_of(g * V7X_SUBLANES, V7X_SUBLANES)
            base = pl.multiple_of(n - V7X_SUBLANES - s0, V7X_SUBLANES)
            win = ebuf[pl.ds(base, n + V7X_SUBLANES), cs]
            zb = z_ref[pl.ds(s0, V7X_SUBLANES), cs]
            for j in range(V7X_SUBLANES):
                acc = acc + win[V7X_SUBLANES - j:V7X_SUBLANES - j + n, :] * zb[j:j + 1, :]
            return acc

        y = lax.fori_loop(0, n // V7X_SUBLANES, body, jnp.zeros((n, V7X_LANES), F32))
        o_ref[:, cs] = (g_ref[:, cs] * (y + z_ref[:, cs] * bias_ref[:, cs])).astype(o_ref.dtype)


def hyena_long_conv_direct(x1, x2, v, full, l1, bias):
    n, ch = v.shape
    z = v
    for o, gate in enumerate((x1, x2)):
        last = o == HYENA_ORDER - 1
        seq = pl.BlockSpec((n, ch), lambda i: (0, 0))
        z = pl.pallas_call(
            functools.partial(_direct_conv_kernel, n=n),
            out_shape=jax.ShapeDtypeStruct((n, ch), BF16 if last else F32),
            grid=(1,),
            in_specs=[pl.BlockSpec((n, ch), lambda i, o=o: (0, o)),
                      pl.BlockSpec((n, ch), lambda i, o=o: (1, o)),
                      pl.BlockSpec((V7X_SUBLANES, ch), lambda i, o=o: (0, o)),
                      seq, seq, pl.BlockSpec((1, ch), lambda i: (0, 0))],
            out_specs=seq,
            scratch_shapes=[pltpu.VMEM((2 * n, ch), F32)],
            compiler_params=_cparams(("arbitrary",), 32 << 20),
            name="hyena_direct_conv",
        )(full, full, l1, z, gate, bias[o:o + 1])
    return z


_DIRECT_CONV_MAX = 512


def hyena_mix(p_c, short_w, short_b, w1, b1, f1, w2, b2, f2, w3, bias):
    n = p_c.shape[0]
    x1, x2, v = hyena_short_conv(p_c, short_w, short_b)
    full, l1 = hyena_filters(n, w1, b1, f1, w2, b2, f2, w3)
    if n <= _DIRECT_CONV_MAX:
        return hyena_long_conv_direct(x1, x2, v, full, l1, bias)
    return hyena_long_conv_fft(x1, x2, v, full, l1, bias)


def _out_kernel(cv_ref, at_ref, hy_ref, w_ref, x_ref, g_ref, gate_ref, o_ref):
    y = jnp.dot(cv_ref[...], w_ref[0:CONV_WIDTH, :], preferred_element_type=F32)
    y = y + jnp.dot(at_ref[...], w_ref[CONV_WIDTH:CONV_WIDTH + MLA_WIDTH, :], preferred_element_type=F32)
    y = y + jnp.dot(hy_ref[...], w_ref[CONV_WIDTH + MLA_WIDTH:, :], preferred_element_type=F32)
    o_ref[...] = x_ref[...] + gate_ref[...] * _rms(y, g_ref[...])


def out_projection(conv_o, attn, hy_o, w_out_bf16, x, g, gate):
    n, d = x.shape
    tm = min(512, n)
    mix = w_out_bf16.shape[0]
    vec = pl.BlockSpec((1, d), lambda i: (0, 0))
    row = lambda w: pl.BlockSpec((tm, w), lambda i: (i, 0))
    return pl.pallas_call(
        _out_kernel,
        out_shape=jax.ShapeDtypeStruct((n, d), F32),
        grid=(n // tm,),
        in_specs=[row(CONV_WIDTH), row(MLA_WIDTH), row(HYENA_WIDTH),
                  _resident((mix, d), lambda i: (0, 0)), row(d), vec, vec],
        out_specs=row(d),
        compiler_params=_cparams(("parallel",), mix * d * 2 + 6 * tm * d * 4 + (8 << 20)),
        name="out_projection",
    )(conv_o, attn, hy_o, w_out_bf16, x, g, gate)


def _ffn_kernel(x_ref, gpre_ref, sh_ref, sc_ref, w1_ref, w2_ref, gpost_ref, gate_ref, o_ref, h_sc):
    f = pl.program_id(1)

    @pl.when(f == 0)
    def _():
        h = _rms(x_ref[...], gpre_ref[...]) * (1.0 + sc_ref[...]) + sh_ref[...]
        h_sc[...] = h.astype(BF16)
        o_ref[...] = jnp.zeros(o_ref.shape, F32)

    a = jnp.dot(h_sc[...], w1_ref[...], preferred_element_type=F32)
    a = jnp.square(jnp.maximum(a, 0.0)).astype(BF16)
    o_ref[...] += jnp.dot(a, w2_ref[...], preferred_element_type=F32)

    @pl.when(f == pl.num_programs(1) - 1)
    def _():
        o_ref[...] = x_ref[...] + gate_ref[...] * _rms(o_ref[...], gpost_ref[...])


def ffn(x, g_pre, shift, scale, w1_bf16, w2_bf16, g_post, gate):
    n, d = x.shape
    dff = w1_bf16.shape[1]
    tm = min(512, n)
    tf = 512
    vec = pl.BlockSpec((1, d), lambda i, f: (0, 0))
    row = pl.BlockSpec((tm, d), lambda i, f: (i, 0))
    vmem = 4 * tm * d * 4 + tm * d * 2 + 4 * d * tf * 2 + 3 * tm * tf * 4 + (8 << 20)
    return pl.pallas_call(
        _ffn_kernel,
        out_shape=jax.ShapeDtypeStruct((n, d), F32),
        grid=(n // tm, dff // tf),
        in_specs=[row, vec, vec, vec,
                  pl.BlockSpec((d, tf), lambda i, f: (0, f)),
                  pl.BlockSpec((tf, d), lambda i, f: (f, 0)), vec, vec],
        out_specs=row,
        scratch_shapes=[pltpu.VMEM((tm, d), BF16)],
        compiler_params=_cparams(("parallel", "arbitrary"), vmem),
        name="ffn",
    )(x, g_pre, shift, scale, w1_bf16, w2_bf16, g_post, gate)


def _rot_cols(w):
    half = w.shape[-1] // 2
    return jnp.concatenate([-w[..., half:], w[..., :half]], axis=-1)


def _rope_table(rows):
    row = jnp.repeat(jnp.arange(rows, dtype=F32), GRID_W)
    col = jnp.tile(jnp.arange(GRID_W, dtype=F32), rows)
    axis_dim = QK_ROPE_DIM // 2
    inv = 1.0 / (ROPE_BASE ** (jnp.arange(0, axis_dim, 2, dtype=F32) / axis_dim))
    ang = jnp.concatenate([row[:, None] * inv, col[:, None] * inv], axis=-1)
    cos, sin = jnp.cos(ang), jnp.sin(ang)
    return jnp.concatenate([cos, cos, sin, sin], axis=-1)


def kernel(x, c, ctx, c_ctx, w_mod, b_mod, g_pre_mix, g_post_mix, g_pre_ffn, g_post_ffn, w_in, conv_dw_w, conv_dw_b, conv_ln_g, conv_ln_b, mla_q_norm, mla_w_uq, mla_kv_norm, mla_w_ukv, hy_short_w, hy_short_b, hy_w1, hy_b1, hy_freq1, hy_w2, hy_b2, hy_freq2, hy_w3, hy_bias, w_out, w_ff1, w_ff2):
    batch, seq, d = x.shape
    assert batch == 1 and c.shape[0] == 1 and ctx.shape[0] == 1
    depth = w_mod.shape[0]
    ctx_len = ctx.shape[1]
    xs, xc = x[0], ctx[0]

    cvec = jnp.zeros((V7X_SUBLANES, d), F32).at[0].set(c[0]).at[1].set(c_ctx)
    mod = modulation(cvec, w_mod, b_mod)

    cs_lat = _rope_table(seq // GRID_W)
    cs_ctx = jnp.concatenate([jnp.ones((ctx_len, QK_ROPE_DIM), F32), jnp.zeros((ctx_len, QK_ROPE_DIM), F32)], -1)
    row = lambda a: a.reshape(1, -1)
    qa, qb_end = Q_LORA_RANK, Q_LORA_RANK + KV_LORA_RANK

    for l in range(depth):
        last = l == depth - 1
        mvec = lambda r, j: mod[l, r:r + 1, j * d:(j + 1) * d]
        wl = w_in[l]
        w_b = wl[:, IN_A:IN_A + IN_B]
        w_kr = w_b[:, qb_end:]
        w_in_p = jnp.concatenate([wl[:, :IN_A], wl[:, IN_A + IN_B:], w_b[:, :qb_end], w_kr, _rot_cols(w_kr)],
                                 axis=-1).astype(BF16)
        wq = mla_w_uq[l].reshape(Q_LORA_RANK, MLA_HEADS, QK_DIM).transpose(1, 0, 2)
        wq = jnp.concatenate([wq, _rot_cols(wq[..., QK_NOPE_DIM:])], axis=-1).astype(BF16)
        wkv = mla_w_ukv[l].reshape(KV_LORA_RANK, MLA_HEADS, QK_NOPE_DIM + V_HEAD_DIM).transpose(1, 0, 2).astype(BF16)
        w_out_b = w_out[l].astype(BF16)
        w1_b, w2_b = w_ff1[l].astype(BF16), w_ff2[l].astype(BF16)

        def project(stream, r, cs):
            p_a, p_c, c_q, c_kv, kr = in_projection(stream, row(g_pre_mix[l]), mvec(r, 0), mvec(r, 1), w_in_p)
            q, k, v = mla_qkv(c_q, c_kv, kr, cs, row(mla_q_norm[l]), row(mla_kv_norm[l]), wq, wkv)
            return p_a, p_c, q, k, v

        def finish(stream, r, p_a, p_c, attn):
            conv_o = conformer_conv(p_a, conv_dw_w[l], row(conv_dw_b[l]), row(conv_ln_g[l]), row(conv_ln_b[l]))
            hy_o = hyena_mix(p_c, hy_short_w[l], row(hy_short_b[l]), hy_w1[l], row(hy_b1[l]), row(hy_freq1[l]),
                             hy_w2[l], row(hy_b2[l]), row(hy_freq2[l]), hy_w3[l], hy_bias[l])
            y = out_projection(conv_o, attn, hy_o, w_out_b, stream, row(g_post_mix[l]), mvec(r, 2))
            return ffn(y, row(g_pre_ffn[l]), mvec(r, 3), mvec(r, 4), w1_b, w2_b, row(g_post_ffn[l]), mvec(r, 5))

        pc_a, pc_c, qc, kc, vc = project(xc, 1, cs_ctx)
        p_a, p_c, q, k, v = project(xs, 0, cs_lat)
        attn = attention(q, [(kc, vc), (k, v)])
        new_x = finish(xs, 0, p_a, p_c, attn)
        if not last:
            xc = finish(xc, 1, pc_a, pc_c, attention(qc, [(kc, vc)]))
        xs = new_x
    return xs[None]
```

```python
import functools
import math

import numpy as np
import jax
import jax.numpy as jnp
from jax import lax
from jax.experimental import pallas as pl
from jax.experimental.pallas import tpu as pltpu

GRID_W = 64
CONV_WIDTH = 512
CONV_TAPS = 31
MLA_HEADS = 8
QK_NOPE_DIM = 128
QK_ROPE_DIM = 64
V_HEAD_DIM = 128
Q_LORA_RANK = 512
KV_LORA_RANK = 256
MLA_WIDTH = MLA_HEADS * V_HEAD_DIM
HYENA_WIDTH = 512
HYENA_ORDER = 2
HYENA_SHORT_TAPS = 3
HYENA_EMB_DIM = 33
HYENA_BANDS = (HYENA_EMB_DIM - 1) // 2
HYENA_FILTER_HIDDEN = 64
HYENA_FAST_DECAY_PCT = 0.3
HYENA_SLOW_DECAY_PCT = 1.5
HYENA_DECAY_TARGET = 1e-2
N_MOD = 6
ROPE_BASE = 10000.0
NORM_EPS = 1e-6
IN_A = 2 * CONV_WIDTH
IN_B = Q_LORA_RANK + KV_LORA_RANK + QK_ROPE_DIM
IN_C = (HYENA_ORDER + 1) * HYENA_WIDTH
QK_DIM = QK_NOPE_DIM + QK_ROPE_DIM

V7X_LANES = 128
V7X_SUBLANES = 8
V7X_VMEM_BYTES = 64 * 1024 * 1024
V7X_VMEM_BUDGET = 56 * 1024 * 1024

FFT_N2 = 256

F32 = jnp.float32
BF16 = jnp.bfloat16
HIGHEST = lax.Precision.HIGHEST


def _cparams(semantics, vmem_bytes):
    return pltpu.CompilerParams(dimension_semantics=semantics,
                                vmem_limit_bytes=int(min(max(vmem_bytes, 16 << 20), V7X_VMEM_BUDGET)))


def _resident(block_shape, index_map):
    return pl.BlockSpec(block_shape, index_map, pipeline_mode=pl.Buffered(1))


def _rms(x, g):
    ms = jnp.mean(x * x, axis=-1, keepdims=True)
    return x * lax.rsqrt(ms + NORM_EPS) * g


def _silu(x):
    return x * jax.nn.sigmoid(x)


def _mod_kernel(cv_ref, w_ref, b_ref, o_ref):
    s = _silu(cv_ref[...])
    o_ref[...] = jnp.dot(s, w_ref[...], preferred_element_type=F32, precision=HIGHEST) + b_ref[...]


def modulation(cvec, w_mod, b_mod):
    n_layers, d, width = w_mod.shape
    tn = 1024
    return pl.pallas_call(
        _mod_kernel,
        out_shape=jax.ShapeDtypeStruct((n_layers, V7X_SUBLANES, width), F32),
        grid=(n_layers, width // tn),
        in_specs=[pl.BlockSpec((V7X_SUBLANES, d), lambda l, j: (0, 0)),
                  pl.BlockSpec((None, d, tn), lambda l, j: (l, 0, j)),
                  pl.BlockSpec((None, 1, tn), lambda l, j: (l, 0, j))],
        out_specs=pl.BlockSpec((None, V7X_SUBLANES, tn), lambda l, j: (l, 0, j)),
        compiler_params=_cparams(("parallel", "parallel"), 2 * d * tn * 4 + (4 << 20)),
        name="modulation",
    )(cvec, w_mod, b_mod.reshape(n_layers, 1, width))


_IN_SPLITS = (IN_A, IN_C, Q_LORA_RANK, KV_LORA_RANK, 2 * QK_ROPE_DIM)
_IN_WIDTH = sum(_IN_SPLITS)


def _in_kernel(x_ref, g_ref, sh_ref, sc_ref, w_ref, *o_refs):
    h = _rms(x_ref[...], g_ref[...]) * (1.0 + sc_ref[...]) + sh_ref[...]
    hb = h.astype(BF16)
    off = 0
    for o_ref, width in zip(o_refs, _IN_SPLITS):
        o_ref[...] = jnp.dot(hb, w_ref[:, off:off + width], preferred_element_type=F32)
        off += width


def in_projection(x, g, shift, scale, w_in_bf16):
    n, d = x.shape
    tm = min(512, n)
    vec = pl.BlockSpec((1, d), lambda i: (0, 0))
    vmem = 2 * tm * d * 4 + d * _IN_WIDTH * 2 + 3 * tm * _IN_WIDTH * 4 + (4 << 20)
    return pl.pallas_call(
        _in_kernel,
        out_shape=[jax.ShapeDtypeStruct((n, w), F32) for w in _IN_SPLITS],
        grid=(n // tm,),
        in_specs=[pl.BlockSpec((tm, d), lambda i: (i, 0)), vec, vec, vec,
                  _resident((d, _IN_WIDTH), lambda i: (0, 0))],
        out_specs=[pl.BlockSpec((tm, w), lambda i: (i, 0)) for w in _IN_SPLITS],
        compiler_params=_cparams(("parallel",), vmem),
        name="in_projection",
    )(x, g, shift, scale, w_in_bf16)


_CONV_HALO = 16
_CONV_ROWS = 32


def _conv_kernel(prev_ref, cur_ref, next_ref, w_ref, b_ref, lg_ref, lb_ref, o_ref, ybuf, *, tile):
    i = pl.program_id(0)
    last = pl.num_programs(0) - 1

    def glu(v):
        return v[:, :CONV_WIDTH] * jax.nn.sigmoid(v[:, CONV_WIDTH:])

    ybuf[0:_CONV_HALO, :] = jnp.where(i > 0, glu(prev_ref[...]), 0.0)
    ybuf[_CONV_HALO:_CONV_HALO + tile, :] = glu(cur_ref[...])
    ybuf[_CONV_HALO + tile:2 * _CONV_HALO + tile, :] = jnp.where(i < last, glu(next_ref[...]), 0.0)

    pad = (CONV_TAPS - 1) // 2
    w = w_ref[...]

    def chunk(r, carry):
        r0 = pl.multiple_of(r * _CONV_ROWS, _CONV_ROWS)
        win = ybuf[pl.ds(r0, _CONV_ROWS + 2 * _CONV_HALO), :]
        acc = jnp.zeros((_CONV_ROWS, CONV_WIDTH), F32)
        for k in range(CONV_TAPS):
            off = _CONV_HALO - pad + k
            acc = acc + win[off:off + _CONV_ROWS, :] * w[k:k + 1, :]
        y = acc + b_ref[...]
        mu = jnp.mean(y, axis=-1, keepdims=True)
        yc = y - mu
        var = jnp.mean(yc * yc, axis=-1, keepdims=True)
        z = yc * lax.rsqrt(var + NORM_EPS) * lg_ref[...] + lb_ref[...]
        o_ref[pl.ds(r0, _CONV_ROWS), :] = _silu(z).astype(o_ref.dtype)
        return carry

    lax.fori_loop(0, tile // _CONV_ROWS, chunk, 0)


def conformer_conv(p_a, dw_w, dw_b, ln_g, ln_b):
    n = p_a.shape[0]
    tile = min(512, n)
    hb = tile // _CONV_HALO
    nhb = n // _CONV_HALO
    vec = pl.BlockSpec((1, CONV_WIDTH), lambda i: (0, 0))
    return pl.pallas_call(
        functools.partial(_conv_kernel, tile=tile),
        out_shape=jax.ShapeDtypeStruct((n, CONV_WIDTH), BF16),
        grid=(n // tile,),
        in_specs=[pl.BlockSpec((_CONV_HALO, IN_A), lambda i: (jnp.maximum(i * hb - 1, 0), 0)),
                  pl.BlockSpec((tile, IN_A), lambda i: (i, 0)),
                  pl.BlockSpec((_CONV_HALO, IN_A), lambda i: (jnp.minimum((i + 1) * hb, nhb - 1), 0)),
                  pl.BlockSpec((CONV_TAPS, CONV_WIDTH), lambda i: (0, 0)), vec, vec, vec],
        out_specs=pl.BlockSpec((tile, CONV_WIDTH), lambda i: (i, 0)),
        scratch_shapes=[pltpu.VMEM((tile + 2 * _CONV_HALO, CONV_WIDTH), F32)],
        compiler_params=_cparams(("parallel",), 4 * tile * IN_A * 4 + (8 << 20)),
        name="conformer_conv",
    )(p_a, p_a, p_a, dw_w, dw_b, ln_g, ln_b)


def _rope_halves(t):
    return t[:, :QK_ROPE_DIM] + t[:, QK_ROPE_DIM:]


_NT = (((1,), (1,)), ((), ()))
_VT_ROWS = V_HEAD_DIM + 16


def _qkv_kernel(cq_ref, ckv_ref, kr_ref, cs_ref, cst_ref, gq_ref, gkv_ref, wqt_ref, wk_ref, wvt_ref,
                qt_ref, k_ref, vt_ref):
    cqn = _rms(cq_ref[...], gq_ref[...]).astype(BF16)
    ckvn = _rms(ckv_ref[...], gkv_ref[...]).astype(BF16)
    cst = cst_ref[...]
    k_rope = _rope_halves(kr_ref[...] * cs_ref[...]).astype(k_ref.dtype)
    ones = jnp.ones((_VT_ROWS - V_HEAD_DIM, cqn.shape[0]), vt_ref.dtype)
    for h in range(MLA_HEADS):
        qa = lax.dot_general(wqt_ref[h], cqn, _NT, preferred_element_type=F32)
        qt_ref[h, 0:QK_NOPE_DIM, :] = qa[:QK_NOPE_DIM].astype(qt_ref.dtype)
        t = qa[QK_NOPE_DIM:] * cst
        qt_ref[h, QK_NOPE_DIM:QK_DIM, :] = (t[:QK_ROPE_DIM] + t[QK_ROPE_DIM:]).astype(qt_ref.dtype)
        k_ref[h, :, 0:QK_NOPE_DIM] = jnp.dot(ckvn, wk_ref[h], preferred_element_type=F32).astype(k_ref.dtype)
        k_ref[h, :, QK_NOPE_DIM:QK_DIM] = k_rope
        vt_ref[h, 0:V_HEAD_DIM, :] = lax.dot_general(wvt_ref[h], ckvn, _NT,
                                                     preferred_element_type=F32).astype(vt_ref.dtype)
        vt_ref[h, V_HEAD_DIM:, :] = ones


def mla_qkv(c_q, c_kv, kr, cs, cst, g_q, g_kv, wqt, wk, wvt):
    n = c_q.shape[0]
    tm = min(512, n)
    row = lambda w: pl.BlockSpec((tm, w), lambda i: (i, 0))
    full = lambda a: pl.BlockSpec(a.shape, lambda i: (0,) * a.ndim)
    return pl.pallas_call(
        _qkv_kernel,
        out_shape=[jax.ShapeDtypeStruct((MLA_HEADS, QK_DIM, n), BF16),
                   jax.ShapeDtypeStruct((MLA_HEADS, n, QK_DIM), BF16),
                   jax.ShapeDtypeStruct((MLA_HEADS, _VT_ROWS, n), BF16)],
        grid=(n // tm,),
        in_specs=[row(Q_LORA_RANK), row(KV_LORA_RANK), row(2 * QK_ROPE_DIM), row(2 * QK_ROPE_DIM),
                  pl.BlockSpec((2 * QK_ROPE_DIM, tm), lambda i: (0, i)),
                  full(g_q), full(g_kv), full(wqt), full(wk), full(wvt)],
        out_specs=[pl.BlockSpec((MLA_HEADS, QK_DIM, tm), lambda i: (0, 0, i)),
                   pl.BlockSpec((MLA_HEADS, tm, QK_DIM), lambda i: (0, i, 0)),
                   pl.BlockSpec((MLA_HEADS, _VT_ROWS, tm), lambda i: (0, 0, i))],
        compiler_params=_cparams(("parallel",), 32 << 20),
        name="mla_qkv",
    )(c_q, c_kv, kr, cs, cst, g_q, g_kv, wqt, wk, wvt)


def _attn_kernel(*refs, chunks):
    n_src = len(chunks)
    qt_ref = refs[0]
    kv_refs = refs[1:1 + 2 * n_src]
    o_ref = refs[1 + 2 * n_src]
    m_sc, acc_sc, st_sc = refs[2 + 2 * n_src:]
    qt = qt_ref[...]
    c = (QK_DIM ** -0.5) * math.log2(math.e)
    m_sc[...] = jnp.full(m_sc.shape, -jnp.inf, F32)
    acc_sc[...] = jnp.zeros(acc_sc.shape, F32)

    def scores(kb):
        return jnp.dot(kb, qt, preferred_element_type=F32)

    def softmax_pv(st, vtb):
        m_prev = m_sc[...]
        m_new = jnp.maximum(m_prev, jnp.max(st, axis=0, keepdims=True))
        alpha = jnp.exp2((m_prev - m_new) * c)
        p = jnp.exp2(((st - m_new) * c).astype(BF16))
        acc_sc[...] = alpha * acc_sc[...] + jnp.dot(vtb, p, preferred_element_type=F32)
        m_sc[...] = m_new

    for s_idx, (length, tk) in enumerate(chunks):
        k_ref, vt_ref = kv_refs[2 * s_idx], kv_refs[2 * s_idx + 1]
        if length == tk:
            softmax_pv(scores(k_ref[...]), vt_ref[...])
            continue
        steps = length // tk
        assert steps % 2 == 0

        def kblk(j, k_ref=k_ref, tk=tk):
            return k_ref[pl.ds(pl.multiple_of(j * tk, tk), tk), :]

        def vblk(j, vt_ref=vt_ref, tk=tk):
            return vt_ref[:, pl.ds(pl.multiple_of(j * tk, tk), tk)]

        st_sc[0] = scores(kblk(0))

        def pair(jj, carry):
            j0 = 2 * jj
            st_sc[1] = scores(kblk(j0 + 1))
            softmax_pv(st_sc[0], vblk(j0))
            st_sc[0] = scores(kblk(j0 + 2))
            softmax_pv(st_sc[1], vblk(j0 + 1))
            return carry

        lax.fori_loop(0, steps // 2 - 1, pair, 0)
        st_sc[1] = scores(kblk(steps - 1))
        softmax_pv(st_sc[0], vblk(steps - 2))
        softmax_pv(st_sc[1], vblk(steps - 1))

    acc = acc_sc[...]
    o = acc[:V_HEAD_DIM] / acc[V_HEAD_DIM:V_HEAD_DIM + 1]
    o_ref[...] = o.T.astype(o_ref.dtype)


_ATTN_TQ = 512
_ATTN_TK = 512


def attention(qt, kv_sources):
    heads, _, lq = qt.shape
    tq = min(_ATTN_TQ, lq)
    chunks = []
    in_specs = [pl.BlockSpec((None, QK_DIM, tq), lambda h, i: (h, 0, i))]
    operands = [qt]
    kv_bytes = 0
    for k, vt in kv_sources:
        lk = k.shape[1]
        chunks.append((lk, min(_ATTN_TK, lk)))
        in_specs.append(pl.BlockSpec((None, lk, QK_DIM), lambda h, i: (h, 0, 0)))
        in_specs.append(pl.BlockSpec((None, _VT_ROWS, lk), lambda h, i: (h, 0, 0)))
        operands += [k, vt]
        kv_bytes += lk * (2 * V7X_LANES + _VT_ROWS) * 2
    return pl.pallas_call(
        functools.partial(_attn_kernel, chunks=tuple(chunks)),
        out_shape=jax.ShapeDtypeStruct((lq, heads * V_HEAD_DIM), BF16),
        grid=(heads, lq // tq),
        in_specs=in_specs,
        out_specs=pl.BlockSpec((tq, V_HEAD_DIM), lambda h, i: (i, h)),
        scratch_shapes=[pltpu.VMEM((1, tq), F32), pltpu.VMEM((_VT_ROWS, tq), F32),
                        pltpu.VMEM((2, _ATTN_TK, tq), F32)],
        compiler_params=_cparams(("parallel", "arbitrary"), 2 * kv_bytes + (16 << 20)),
        name="attention",
    )(*operands)


_SHORT_HALO = 8


def _short_kernel(prev_ref, cur_ref, next_ref, w_ref, b_ref, x1_ref, x2_ref, v_ref, ubuf, *, tile):
    i = pl.program_id(0)
    last = pl.num_programs(0) - 1
    ubuf[0:_SHORT_HALO, :] = jnp.where(i > 0, prev_ref[...], 0.0)
    ubuf[_SHORT_HALO:_SHORT_HALO + tile, :] = cur_ref[...]
    ubuf[_SHORT_HALO + tile:2 * _SHORT_HALO + tile, :] = jnp.where(i < last, next_ref[...], 0.0)
    pad = (HYENA_SHORT_TAPS - 1) // 2
    w = w_ref[...]
    for o_ref, c0 in ((x1_ref, 0), (x2_ref, HYENA_WIDTH), (v_ref, 2 * HYENA_WIDTH)):
        acc = jnp.zeros((tile, HYENA_WIDTH), F32) + b_ref[:, c0:c0 + HYENA_WIDTH]
        for k in range(HYENA_SHORT_TAPS):
            acc = acc + (ubuf[_SHORT_HALO - pad + k:_SHORT_HALO - pad + k + tile, c0:c0 + HYENA_WIDTH]
                         * w[k:k + 1, c0:c0 + HYENA_WIDTH])
        o_ref[...] = acc


def hyena_short_conv(p_c, w, b):
    n = p_c.shape[0]
    tile = min(256, n)
    hb = tile // _SHORT_HALO
    nhb = n // _SHORT_HALO
    out = jax.ShapeDtypeStruct((n, HYENA_WIDTH), F32)
    o_spec = pl.BlockSpec((tile, HYENA_WIDTH), lambda i: (i, 0))
    return pl.pallas_call(
        functools.partial(_short_kernel, tile=tile),
        out_shape=[out, out, out],
        grid=(n // tile,),
        in_specs=[pl.BlockSpec((_SHORT_HALO, IN_C), lambda i: (jnp.maximum(i * hb - 1, 0), 0)),
                  pl.BlockSpec((tile, IN_C), lambda i: (i, 0)),
                  pl.BlockSpec((_SHORT_HALO, IN_C), lambda i: (jnp.minimum((i + 1) * hb, nhb - 1), 0)),
                  pl.BlockSpec((HYENA_SHORT_TAPS, IN_C), lambda i: (0, 0)),
                  pl.BlockSpec((1, IN_C), lambda i: (0, 0))],
        out_specs=[o_spec, o_spec, o_spec],
        scratch_shapes=[pltpu.VMEM((tile + 2 * _SHORT_HALO, IN_C), F32)],
        compiler_params=_cparams(("parallel",), 32 << 20),
        name="hyena_short_conv",
    )(p_c, p_c, p_c, w, b)


_FILT_W = HYENA_ORDER * HYENA_WIDTH
_EMB_PAD = 128


def _filter_kernel(emb_ref, w1_ref, b1_ref, f1_ref, w2_ref, b2_ref, f2_ref, w3_ref, dl_ref,
                   full_ref, l1_ref, *, tiles_per_dir):
    i = pl.program_id(0)
    emb = emb_ref[...]
    hid = jnp.sin(f1_ref[...] * (jnp.dot(emb, w1_ref[...], preferred_element_type=F32, precision=HIGHEST)
                                 + b1_ref[...]))
    hid = jnp.sin(f2_ref[...] * (jnp.dot(hid, w2_ref[...], preferred_element_type=F32, precision=HIGHEST)
                                 + b2_ref[...]))
    filt = jnp.dot(hid, w3_ref[...], preferred_element_type=F32, precision=HIGHEST)
    decay = jnp.exp(-emb[:, 0:1] * dl_ref[...])
    rows = lax.broadcasted_iota(jnp.int32, (filt.shape[0], 1), 0)
    keep = jnp.logical_or(i != tiles_per_dir, rows != 0)
    for o in range(HYENA_ORDER):
        sl = slice(o * HYENA_WIDTH, (o + 1) * HYENA_WIDTH)
        full_ref[:, sl] = jnp.where(keep, filt[:, sl] * decay, 0.0)

    @pl.when(i == 0)
    def _():
        l1_ref[...] = jnp.zeros(l1_ref.shape, F32)

    l1_ref[0:1, :] += jnp.sum(jnp.abs(full_ref[...]), axis=0, keepdims=True)


def hyena_filters(n, w1, b1, f1, w2, b2, f2, w3):
    pos = jnp.arange(n + 1, dtype=F32)[:, None]
    t = jnp.concatenate([jnp.linspace(0.0, 1.0, n, dtype=F32), jnp.ones((1,), F32)])[:, None]
    wpos = 2.0 * math.pi * pos / n
    f = jnp.linspace(1e-4, HYENA_BANDS - 1, HYENA_BANDS, dtype=F32)[None, :]
    emb = jnp.concatenate([t, jnp.cos(f * wpos), -jnp.sin(f * wpos)], axis=-1)
    emb_full = jnp.concatenate([emb[:n], emb[n:0:-1]], axis=0)
    emb_full = jnp.pad(emb_full, ((0, 0), (0, _EMB_PAD - HYENA_EMB_DIM)))
    w1p = jnp.pad(w1, ((0, _EMB_PAD - HYENA_EMB_DIM), (0, 0)))
    w3d = w3.reshape(HYENA_FILTER_HIDDEN, 2, _FILT_W).transpose(1, 0, 2)
    min_decay = math.log(HYENA_DECAY_TARGET) / HYENA_SLOW_DECAY_PCT
    max_decay = math.log(HYENA_DECAY_TARGET) / HYENA_FAST_DECAY_PCT
    deltas = jnp.abs(jnp.linspace(min_decay, max_decay, HYENA_WIDTH, dtype=F32))[None, :]
    tile = min(512, n)
    tiles_per_dir = n // tile
    hv = pl.BlockSpec((1, HYENA_FILTER_HIDDEN), lambda i: (0, 0))
    return pl.pallas_call(
        functools.partial(_filter_kernel, tiles_per_dir=tiles_per_dir),
        out_shape=[jax.ShapeDtypeStruct((2 * n, _FILT_W), F32),
                   jax.ShapeDtypeStruct((V7X_SUBLANES, _FILT_W), F32)],
        grid=(2 * tiles_per_dir,),
        in_specs=[pl.BlockSpec((tile, _EMB_PAD), lambda i: (i, 0)),
                  pl.BlockSpec((_EMB_PAD, HYENA_FILTER_HIDDEN), lambda i: (0, 0)), hv, hv,
                  pl.BlockSpec((HYENA_FILTER_HIDDEN, HYENA_FILTER_HIDDEN), lambda i: (0, 0)), hv, hv,
                  pl.BlockSpec((None, HYENA_FILTER_HIDDEN, _FILT_W), lambda i: (i // tiles_per_dir, 0, 0)),
                  pl.BlockSpec((1, HYENA_WIDTH), lambda i: (0, 0))],
        out_specs=[pl.BlockSpec((tile, _FILT_W), lambda i: (i, 0)),
                   pl.BlockSpec((V7X_SUBLANES, _FILT_W), lambda i: (0, 0))],
        compiler_params=_cparams(("arbitrary",), 32 << 20),
        name="hyena_filters",
    )(emb_full, w1p, b1, f1, w2, b2, f2, w3d, deltas)


def _fft_tables(n1):
    n2 = FFT_N2
    nb = n1 // 2 + 1
    nbp = -(-nb // V7X_SUBLANES) * V7X_SUBLANES
    k1 = np.arange(nbp)[:, None].astype(np.float64)
    valid = (np.arange(nbp) < nb)[:, None]
    s1 = np.arange(n1)[None, :]
    ang = 2.0 * np.pi * k1 * s1 / n1
    f1 = np.concatenate([np.where(valid, np.cos(ang), 0.0), np.where(valid, -np.sin(ang), 0.0)], axis=0)
    idx = np.arange(n2)
    ang2 = 2.0 * np.pi * np.outer(idx, idx) / n2
    f2 = np.stack([np.cos(ang2), -np.sin(ang2)])
    angt = 2.0 * np.pi * k1 * idx[None, :] / (n1 * n2)
    tw_f = np.stack([np.cos(angt), -np.sin(angt)])[:, :, None, :]
    t1 = np.arange(n1 // 2)[:, None]
    ang1 = 2.0 * np.pi * t1 * np.arange(nbp)[None, :] / n1
    inv1 = np.stack([np.cos(ang1), np.sin(ang1)])
    wk = np.where((np.arange(nbp) == 0) | (np.arange(nbp) == n1 // 2), 1.0, 2.0) * (np.arange(nbp) < nb)
    angb = 2.0 * np.pi * idx[:, None] * np.arange(nbp)[None, :] / (n1 * n2)
    tw_i = np.stack([np.cos(angb) * wk, np.sin(angb) * wk], axis=1) / (n1 * n2)
    as32 = lambda a: jnp.asarray(a, dtype=F32)
    return dict(nb=nb, nbp=nbp, f1=as32(f1), f2=as32(f2), tw_f=as32(tw_f), inv1=as32(inv1), tw_i=as32(tw_i))


def _fft1_kernel(x_ref, f_ref, o_ref):
    nbp = o_ref.shape[1]
    a = jnp.dot(f_ref[...], x_ref[...], preferred_element_type=F32, precision=HIGHEST)
    o_ref[0] = a[:nbp]
    o_ref[1] = a[nbp:]


def fft_level1(x, f1, n1):
    rows, ch = x.shape
    s1 = rows // FFT_N2
    cols = FFT_N2 * ch
    nbp = f1.shape[0] // 2
    tc = min(8192, cols)
    out = pl.pallas_call(
        _fft1_kernel,
        out_shape=jax.ShapeDtypeStruct((2, nbp, cols), F32),
        grid=(cols // tc,),
        in_specs=[pl.BlockSpec((s1, tc), lambda j: (0, j)),
                  pl.BlockSpec((2 * nbp, s1), lambda j: (0, 0))],
        out_specs=pl.BlockSpec((2, nbp, tc), lambda j: (0, 0, j)),
        compiler_params=_cparams(("parallel",), 2 * (s1 + 4 * nbp) * tc * 4 + (8 << 20)),
        name="fft_level1",
    )(x.reshape(s1, cols), f1[:, :s1])
    return out.reshape(2, nbp, FFT_N2, ch)


def _cdot(ar, ai, br, bi):
    dot = functools.partial(jnp.dot, preferred_element_type=F32, precision=HIGHEST)
    return dot(ar, br) - dot(ai, bi), dot(ar, bi) + dot(ai, br)


def _twiddled_dft(f_ref, t_ref):
    fr, fi = f_ref[0], f_ref[1]
    tr, ti = t_ref[0], t_ref[1]
    return fr * tr - fi * ti, fr * ti + fi * tr


def _fft2_spectrum_kernel(a_ref, f_ref, t_ref, l1_ref, o_ref, *, nb):
    k1 = pl.program_id(0)

    @pl.when(k1 < nb)
    def _():
        gr, gi = _twiddled_dft(f_ref, t_ref)
        xr, xi = _cdot(gr, gi, a_ref[0], a_ref[1])
        inv = 1.0 / l1_ref[0:1, :]
        o_ref[0] = xr * inv
        o_ref[1] = xi * inv

    @pl.when(k1 >= nb)
    def _():
        o_ref[...] = jnp.zeros(o_ref.shape, F32)


def fft_level2_spectrum(a, tabs, l1):
    _, nbp, n2, ch = a.shape
    blk = pl.BlockSpec((2, None, n2, ch), lambda k: (0, k, 0, 0))
    return pl.pallas_call(
        functools.partial(_fft2_spectrum_kernel, nb=tabs["nb"]),
        out_shape=jax.ShapeDtypeStruct(a.shape, F32),
        grid=(nbp,),
        in_specs=[blk, pl.BlockSpec((2, n2, n2), lambda k: (0, 0, 0)),
                  pl.BlockSpec((2, None, 1, n2), lambda k: (0, k, 0, 0)),
                  pl.BlockSpec((V7X_SUBLANES, ch), lambda k: (0, 0))],
        out_specs=blk,
        compiler_params=_cparams(("parallel",), 8 * 2 * n2 * ch * 4 + (8 << 20)),
        name="fft_level2_spectrum",
    )(a, tabs["f2"], tabs["tw_f"], l1)


def _fft2_conv_kernel(a_ref, kf_ref, f_ref, t_ref, o_ref, *, nb):
    k1 = pl.program_id(0)

    @pl.when(k1 < nb)
    def _():
        gr, gi = _twiddled_dft(f_ref, t_ref)
        xr, xi = _cdot(gr, gi, a_ref[0], a_ref[1])
        kr, ki = kf_ref[0], kf_ref[1]
        yr = xr * kr - xi * ki
        yi = xr * ki + xi * kr
        br, bi = _cdot(f_ref[0], -f_ref[1], yr, yi)
        o_ref[0] = br
        o_ref[1] = bi

    @pl.when(k1 >= nb)
    def _():
        o_ref[...] = jnp.zeros(o_ref.shape, F32)


def fft_level2_conv(a, kf, order, tabs):
    _, nbp, n2, ch = a.shape
    blk = pl.BlockSpec((2, None, n2, ch), lambda k: (0, k, 0, 0))
    return pl.pallas_call(
        functools.partial(_fft2_conv_kernel, nb=tabs["nb"]),
        out_shape=jax.ShapeDtypeStruct(a.shape, F32),
        grid=(nbp,),
        in_specs=[blk, pl.BlockSpec((2, None, n2, ch), lambda k: (0, k, 0, order)),
                  pl.BlockSpec((2, n2, n2), lambda k: (0, 0, 0)),
                  pl.BlockSpec((2, None, 1, n2), lambda k: (0, k, 0, 0))],
        out_specs=blk,
        compiler_params=_cparams(("parallel",), 10 * 2 * n2 * ch * 4 + (8 << 20)),
        name="fft_level2_conv",
    )(a, kf, tabs["f2"], tabs["tw_f"])


_INV_T2 = 8


def _ifft1_gate_kernel(b_ref, inv_ref, tw_ref, z_ref, g_ref, bias_ref, o_ref):
    ca, sa = inv_ref[0], inv_ref[1]
    for j in range(_INV_T2):
        cb, sb = tw_ref[j, 0:1, :], tw_ref[j, 1:2, :]
        mc = ca * cb - sa * sb
        ms = -(sa * cb + ca * sb)
        y = (jnp.dot(mc, b_ref[0, :, j, :], preferred_element_type=F32, precision=HIGHEST)
             + jnp.dot(ms, b_ref[1, :, j, :], preferred_element_type=F32, precision=HIGHEST))
        z = z_ref[:, j, :]
        o_ref[:, j, :] = (g_ref[:, j, :] * (y + z * bias_ref[...])).astype(o_ref.dtype)


def ifft_level1_gate(b, z, gate, bias, tabs, out_dtype):
    _, nbp, n2, ch = b.shape
    n = z.shape[0]
    s1 = n // n2
    view = lambda a: a.reshape(s1, n2, ch)
    seq = pl.BlockSpec((s1, _INV_T2, ch), lambda j: (0, j, 0))
    out = pl.pallas_call(
        _ifft1_gate_kernel,
        out_shape=jax.ShapeDtypeStruct((s1, n2, ch), out_dtype),
        grid=(n2 // _INV_T2,),
        in_specs=[pl.BlockSpec((2, nbp, _INV_T2, ch), lambda j: (0, 0, j, 0)),
                  pl.BlockSpec((2, s1, nbp), lambda j: (0, 0, 0)),
                  pl.BlockSpec((_INV_T2, 2, nbp), lambda j: (j, 0, 0)),
                  seq, seq, pl.BlockSpec((1, ch), lambda j: (0, 0))],
        out_specs=seq,
        compiler_params=_cparams(("parallel",), 2 * (2 * nbp + 3 * s1) * _INV_T2 * ch * 4 + (8 << 20)),
        name="ifft_level1_gate",
    )(b, tabs["inv1"], tabs["tw_i"], view(z), view(gate), bias)
    return out.reshape(n, ch)


def hyena_long_conv_fft(x1, x2, v, full, l1, bias):
    n, ch = v.shape
    n1 = 2 * n // FFT_N2
    tabs = _fft_tables(n1)
    kf = fft_level2_spectrum(fft_level1(full, tabs["f1"], n1), tabs, l1)
    z = v
    for o, gate in enumerate((x1, x2)):
        a = fft_level1(z, tabs["f1"], n1)
        b = fft_level2_conv(a, kf, o, tabs)
        last = o == HYENA_ORDER - 1
        z = ifft_level1_gate(b, z, gate, bias[o:o + 1], tabs, BF16 if last else F32)
    return z


def _direct_conv_kernel(lo_ref, hi_ref, l1_ref, z_ref, g_ref, bias_ref, o_ref, ebuf, *, n):
    inv = 1.0 / l1_ref[0:1, :]
    ebuf[0:n, :] = hi_ref[...] * inv
    ebuf[n:2 * n, :] = lo_ref[...] * inv
    for c0 in range(0, HYENA_WIDTH, V7X_LANES):
        cs = slice(c0, c0 + V7X_LANES)

        def body(g, acc, cs=cs):
            s0 = pl.multiple_of(g * V7X_SUBLANES, V7X_SUBLANES)
            base = pl.multiple_of(n - V7X_SUBLANES - s0, V7X_SUBLANES)
            win = ebuf[pl.ds(base, n + V7X_SUBLANES), cs]
            zb = z_ref[pl.ds(s0, V7X_SUBLANES), cs]
            for j in range(V7X_SUBLANES):
                acc = acc + win[V7X_SUBLANES - j:V7X_SUBLANES - j + n, :] * zb[j:j + 1, :]
            return acc

        y = lax.fori_loop(0, n // V7X_SUBLANES, body, jnp.zeros((n, V7X_LANES), F32))
        o_ref[:, cs] = (g_ref[:, cs] * (y + z_ref[:, cs] * bias_ref[:, cs])).astype(o_ref.dtype)


def hyena_long_conv_direct(x1, x2, v, full, l1, bias):
    n, ch = v.shape
    z = v
    for o, gate in enumerate((x1, x2)):
        last = o == HYENA_ORDER - 1
        seq = pl.BlockSpec((n, ch), lambda i: (0, 0))
        z = pl.pallas_call(
            functools.partial(_direct_conv_kernel, n=n),
            out_shape=jax.ShapeDtypeStruct((n, ch), BF16 if last else F32),
            grid=(1,),
            in_specs=[pl.BlockSpec((n, ch), lambda i, o=o: (0, o)),
                      pl.BlockSpec((n, ch), lambda i, o=o: (1, o)),
                      pl.BlockSpec((V7X_SUBLANES, ch), lambda i, o=o: (0, o)),
                      seq, seq, pl.BlockSpec((1, ch), lambda i: (0, 0))],
            out_specs=seq,
            scratch_shapes=[pltpu.VMEM((2 * n, ch), F32)],
            compiler_params=_cparams(("arbitrary",), 32 << 20),
            name="hyena_direct_conv",
        )(full, full, l1, z, gate, bias[o:o + 1])
    return z


_DIRECT_CONV_MAX = 512


def hyena_mix(p_c, short_w, short_b, w1, b1, f1, w2, b2, f2, w3, bias):
    n = p_c.shape[0]
    x1, x2, v = hyena_short_conv(p_c, short_w, short_b)
    full, l1 = hyena_filters(n, w1, b1, f1, w2, b2, f2, w3)
    if n <= _DIRECT_CONV_MAX:
        return hyena_long_conv_direct(x1, x2, v, full, l1, bias)
    return hyena_long_conv_fft(x1, x2, v, full, l1, bias)


def _out_kernel(cv_ref, at_ref, hy_ref, w_ref, x_ref, g_ref, gate_ref, o_ref):
    y = jnp.dot(cv_ref[...], w_ref[0:CONV_WIDTH, :], preferred_element_type=F32)
    y = y + jnp.dot(at_ref[...], w_ref[CONV_WIDTH:CONV_WIDTH + MLA_WIDTH, :], preferred_element_type=F32)
    y = y + jnp.dot(hy_ref[...], w_ref[CONV_WIDTH + MLA_WIDTH:, :], preferred_element_type=F32)
    o_ref[...] = x_ref[...] + gate_ref[...] * _rms(y, g_ref[...])


def out_projection(conv_o, attn, hy_o, w_out_bf16, x, g, gate):
    n, d = x.shape
    tm = min(512, n)
    mix = w_out_bf16.shape[0]
    vec = pl.BlockSpec((1, d), lambda i: (0, 0))
    row = lambda w: pl.BlockSpec((tm, w), lambda i: (i, 0))
    return pl.pallas_call(
        _out_kernel,
        out_shape=jax.ShapeDtypeStruct((n, d), F32),
        grid=(n // tm,),
        in_specs=[row(CONV_WIDTH), row(MLA_WIDTH), row(HYENA_WIDTH),
                  _resident((mix, d), lambda i: (0, 0)), row(d), vec, vec],
        out_specs=row(d),
        compiler_params=_cparams(("parallel",), mix * d * 2 + 6 * tm * d * 4 + (8 << 20)),
        name="out_projection",
    )(conv_o, attn, hy_o, w_out_bf16, x, g, gate)


def _ffn_kernel(x_ref, gpre_ref, sh_ref, sc_ref, w1_ref, w2_ref, gpost_ref, gate_ref, o_ref, h_sc):
    f = pl.program_id(1)

    @pl.when(f == 0)
    def _():
        h = _rms(x_ref[...], gpre_ref[...]) * (1.0 + sc_ref[...]) + sh_ref[...]
        h_sc[...] = h.astype(BF16)
        o_ref[...] = jnp.zeros(o_ref.shape, F32)

    a = jnp.dot(h_sc[...], w1_ref[...], preferred_element_type=F32)
    a = jnp.square(jnp.maximum(a, 0.0)).astype(BF16)
    o_ref[...] += jnp.dot(a, w2_ref[...], preferred_element_type=F32)

    @pl.when(f == pl.num_programs(1) - 1)
    def _():
        o_ref[...] = x_ref[...] + gate_ref[...] * _rms(o_ref[...], gpost_ref[...])


def ffn(x, g_pre, shift, scale, w1_bf16, w2_bf16, g_post, gate):
    n, d = x.shape
    dff = w1_bf16.shape[1]
    tm = min(512, n)
    tf = 512
    vec = pl.BlockSpec((1, d), lambda i, f: (0, 0))
    row = pl.BlockSpec((tm, d), lambda i, f: (i, 0))
    vmem = 4 * tm * d * 4 + tm * d * 2 + 4 * d * tf * 2 + 3 * tm * tf * 4 + (8 << 20)
    return pl.pallas_call(
        _ffn_kernel,
        out_shape=jax.ShapeDtypeStruct((n, d), F32),
        grid=(n // tm, dff // tf),
        in_specs=[row, vec, vec, vec,
                  pl.BlockSpec((d, tf), lambda i, f: (0, f)),
                  pl.BlockSpec((tf, d), lambda i, f: (f, 0)), vec, vec],
        out_specs=row,
        scratch_shapes=[pltpu.VMEM((tm, d), BF16)],
        compiler_params=_cparams(("parallel", "arbitrary"), vmem),
        name="ffn",
    )(x, g_pre, shift, scale, w1_bf16, w2_bf16, g_post, gate)


def _rot_cols(w):
    half = w.shape[-1] // 2
    return jnp.concatenate([-w[..., half:], w[..., :half]], axis=-1)


def _rope_table(rows):
    row = jnp.repeat(jnp.arange(rows, dtype=F32), GRID_W)
    col = jnp.tile(jnp.arange(GRID_W, dtype=F32), rows)
    axis_dim = QK_ROPE_DIM // 2
    inv = 1.0 / (ROPE_BASE ** (jnp.arange(0, axis_dim, 2, dtype=F32) / axis_dim))
    ang = jnp.concatenate([row[:, None] * inv, col[:, None] * inv], axis=-1)
    cos, sin = jnp.cos(ang), jnp.sin(ang)
    return jnp.concatenate([cos, cos, sin, sin], axis=-1)


def kernel(x, c, ctx, c_ctx, w_mod, b_mod, g_pre_mix, g_post_mix, g_pre_ffn, g_post_ffn, w_in, conv_dw_w, conv_dw_b, conv_ln_g, conv_ln_b, mla_q_norm, mla_w_uq, mla_kv_norm, mla_w_ukv, hy_short_w, hy_short_b, hy_w1, hy_b1, hy_freq1, hy_w2, hy_b2, hy_freq2, hy_w3, hy_bias, w_out, w_ff1, w_ff2):
    batch, seq, d = x.shape
    assert batch == 1 and c.shape[0] == 1 and ctx.shape[0] == 1
    depth = w_mod.shape[0]
    ctx_len = ctx.shape[1]
    xs, xc = x[0], ctx[0]

    cvec = jnp.zeros((V7X_SUBLANES, d), F32).at[0].set(c[0]).at[1].set(c_ctx)
    mod = modulation(cvec, w_mod, b_mod)

    cs_lat = _rope_table(seq // GRID_W)
    cs_ctx = jnp.concatenate([jnp.ones((ctx_len, QK_ROPE_DIM), F32), jnp.zeros((ctx_len, QK_ROPE_DIM), F32)], -1)
    row = lambda a: a.reshape(1, -1)
    qa, qb_end = Q_LORA_RANK, Q_LORA_RANK + KV_LORA_RANK

    for l in range(depth):
        last = l == depth - 1
        mvec = lambda r, j: mod[l, r:r + 1, j * d:(j + 1) * d]
        wl = w_in[l]
        w_b = wl[:, IN_A:IN_A + IN_B]
        w_kr = w_b[:, qb_end:]
        w_in_p = jnp.concatenate([wl[:, :IN_A], wl[:, IN_A + IN_B:], w_b[:, :qb_end], w_kr, _rot_cols(w_kr)],
                                 axis=-1).astype(BF16)
        wq = mla_w_uq[l].reshape(Q_LORA_RANK, MLA_HEADS, QK_DIM).transpose(1, 0, 2)
        wq = jnp.concatenate([wq, _rot_cols(wq[..., QK_NOPE_DIM:])], axis=-1)
        wqt = wq.transpose(0, 2, 1).astype(BF16)
        wkv = mla_w_ukv[l].reshape(KV_LORA_RANK, MLA_HEADS, QK_NOPE_DIM + V_HEAD_DIM).transpose(1, 0, 2)
        wk = wkv[..., :QK_NOPE_DIM].astype(BF16)
        wvt = wkv[..., QK_NOPE_DIM:].transpose(0, 2, 1).astype(BF16)
        w_out_b = w_out[l].astype(BF16)
        w1_b, w2_b = w_ff1[l].astype(BF16), w_ff2[l].astype(BF16)

        def project(stream, r, cs):
            p_a, p_c, c_q, c_kv, kr = in_projection(stream, row(g_pre_mix[l]), mvec(r, 0), mvec(r, 1), w_in_p)
            qt, k, vt = mla_qkv(c_q, c_kv, kr, cs, cs.T, row(mla_q_norm[l]), row(mla_kv_norm[l]), wqt, wk, wvt)
            return p_a, p_c, qt, k, vt

        def finish(stream, r, p_a, p_c, attn):
            conv_o = conformer_conv(p_a, conv_dw_w[l], row(conv_dw_b[l]), row(conv_ln_g[l]), row(conv_ln_b[l]))
            hy_o = hyena_mix(p_c, hy_short_w[l], row(hy_short_b[l]), hy_w1[l], row(hy_b1[l]), row(hy_freq1[l]),
                             hy_w2[l], row(hy_b2[l]), row(hy_freq2[l]), hy_w3[l], hy_bias[l])
            y = out_projection(conv_o, attn, hy_o, w_out_b, stream, row(g_post_mix[l]), mvec(r, 2))
            return ffn(y, row(g_pre_ffn[l]), mvec(r, 3), mvec(r, 4), w1_b, w2_b, row(g_post_ffn[l]), mvec(r, 5))

        pc_a, pc_c, qc, kc, vc = project(xc, 1, cs_ctx)
        p_a, p_c, q, k, v = project(xs, 0, cs_lat)
        attn = attention(q, [(kc, vc), (k, v)])
        new_x = finish(xs, 0, p_a, p_c, attn)
        if not last:
            xc = finish(xc, 1, pc_a, pc_c, attention(qc, [(kc, vc)]))
        xs = new_x
    return xs[None]
```

```python
import functools
import math

import numpy as np
import jax
import jax.numpy as jnp
from jax import lax
from jax.experimental import pallas as pl
from jax.experimental.pallas import tpu as pltpu

GRID_W = 64
CONV_WIDTH = 512
CONV_TAPS = 31
MLA_HEADS = 8
QK_NOPE_DIM = 128
QK_ROPE_DIM = 64
V_HEAD_DIM = 128
Q_LORA_RANK = 512
KV_LORA_RANK = 256
MLA_WIDTH = MLA_HEADS * V_HEAD_DIM
HYENA_WIDTH = 512
HYENA_ORDER = 2
HYENA_SHORT_TAPS = 3
HYENA_EMB_DIM = 33
HYENA_BANDS = (HYENA_EMB_DIM - 1) // 2
HYENA_FILTER_HIDDEN = 64
HYENA_FAST_DECAY_PCT = 0.3
HYENA_SLOW_DECAY_PCT = 1.5
HYENA_DECAY_TARGET = 1e-2
N_MOD = 6
ROPE_BASE = 10000.0
NORM_EPS = 1e-6
IN_A = 2 * CONV_WIDTH
IN_B = Q_LORA_RANK + KV_LORA_RANK + QK_ROPE_DIM
IN_C = (HYENA_ORDER + 1) * HYENA_WIDTH
QK_DIM = QK_NOPE_DIM + QK_ROPE_DIM

V7X_LANES = 128
V7X_SUBLANES = 8
V7X_VMEM_BYTES = 64 * 1024 * 1024
V7X_VMEM_BUDGET = 56 * 1024 * 1024

FFT_N2 = 256

F32 = jnp.float32
BF16 = jnp.bfloat16
HIGHEST = lax.Precision.HIGHEST


def _cparams(semantics, vmem_bytes):
    return pltpu.CompilerParams(dimension_semantics=semantics,
                                vmem_limit_bytes=int(min(max(vmem_bytes, 16 << 20), V7X_VMEM_BUDGET)))


def _resident(block_shape, index_map):
    return pl.BlockSpec(block_shape, index_map, pipeline_mode=pl.Buffered(1))


def _rms(x, g):
    ms = jnp.mean(x * x, axis=-1, keepdims=True)
    return x * lax.rsqrt(ms + NORM_EPS) * g


def _silu(x):
    return x * jax.nn.sigmoid(x)


_MOD_VECS = 2


def _mod_kernel(cb_ref, w_ref, b_ref, o_ref, s_sc):
    d, tn = w_ref.shape
    reps = tn // V7X_LANES

    @pl.when(jnp.logical_and(pl.program_id(0) == 0, pl.program_id(1) == 0))
    def _():
        s_sc[...] = _silu(cb_ref[...])

    def body(kb, accs):
        k0 = pl.multiple_of(kb * V7X_SUBLANES, V7X_SUBLANES)
        wblk = w_ref[pl.ds(k0, V7X_SUBLANES), :]
        out = []
        for r in range(_MOD_VECS):
            s = s_sc[r, pl.ds(k0, V7X_SUBLANES), :]
            out.append(accs[r] + wblk * jnp.concatenate([s] * reps, axis=1))
        return tuple(out)

    zero = jnp.zeros((V7X_SUBLANES, tn), F32)
    accs = lax.fori_loop(0, d // V7X_SUBLANES, body, (zero,) * _MOD_VECS)
    o_ref[...] = jnp.zeros(o_ref.shape, F32)
    for r in range(_MOD_VECS):
        o_ref[r:r + 1, :] = jnp.sum(accs[r], axis=0, keepdims=True) + b_ref[...]


def modulation(cvecs, w_mod, b_mod):
    n_layers, d, width = w_mod.shape
    tn = 1024
    cb = jnp.broadcast_to(cvecs[:, :, None], (_MOD_VECS, d, V7X_LANES))
    return pl.pallas_call(
        _mod_kernel,
        out_shape=jax.ShapeDtypeStruct((n_layers, V7X_SUBLANES, width), F32),
        grid=(n_layers, width // tn),
        in_specs=[pl.BlockSpec((_MOD_VECS, d, V7X_LANES), lambda l, j: (0, 0, 0)),
                  pl.BlockSpec((None, d, tn), lambda l, j: (l, 0, j)),
                  pl.BlockSpec((None, 1, tn), lambda l, j: (l, 0, j))],
        out_specs=pl.BlockSpec((None, V7X_SUBLANES, tn), lambda l, j: (l, 0, j)),
        scratch_shapes=[pltpu.VMEM((_MOD_VECS, d, V7X_LANES), F32)],
        compiler_params=_cparams(("arbitrary", "arbitrary"), 2 * d * tn * 4 + (8 << 20)),
        name="modulation",
    )(cb, w_mod, b_mod.reshape(n_layers, 1, width))


_IN_SPLITS = (IN_A, IN_C, Q_LORA_RANK, KV_LORA_RANK, 2 * QK_ROPE_DIM)
_IN_WIDTH = sum(_IN_SPLITS)


def _in_kernel(x_ref, g_ref, sh_ref, sc_ref, w_ref, *o_refs):
    h = _rms(x_ref[...], g_ref[...]) * (1.0 + sc_ref[...]) + sh_ref[...]
    hb = h.astype(BF16)
    off = 0
    for o_ref, width in zip(o_refs, _IN_SPLITS):
        o_ref[...] = jnp.dot(hb, w_ref[:, off:off + width], preferred_element_type=F32)
        off += width


def in_projection(x, g, shift, scale, w_in_bf16):
    n, d = x.shape
    tm = min(512, n)
    vec = pl.BlockSpec((1, d), lambda i: (0, 0))
    vmem = 2 * tm * d * 4 + d * _IN_WIDTH * 2 + 3 * tm * _IN_WIDTH * 4 + (4 << 20)
    return pl.pallas_call(
        _in_kernel,
        out_shape=[jax.ShapeDtypeStruct((n, w), F32) for w in _IN_SPLITS],
        grid=(n // tm,),
        in_specs=[pl.BlockSpec((tm, d), lambda i: (i, 0)), vec, vec, vec,
                  _resident((d, _IN_WIDTH), lambda i: (0, 0))],
        out_specs=[pl.BlockSpec((tm, w), lambda i: (i, 0)) for w in _IN_SPLITS],
        compiler_params=_cparams(("parallel",), vmem),
        name="in_projection",
    )(x, g, shift, scale, w_in_bf16)


_CONV_HALO = 16
_CONV_ROWS = 32


def _conv_kernel(prev_ref, cur_ref, next_ref, w_ref, b_ref, lg_ref, lb_ref, o_ref, ybuf, cbuf, *, tile):
    i = pl.program_id(0)
    last = pl.num_programs(0) - 1

    def glu(v):
        return v[:, :CONV_WIDTH] * jax.nn.sigmoid(v[:, CONV_WIDTH:])

    ybuf[0, 0:_CONV_HALO, :] = jnp.where(i > 0, glu(prev_ref[...]), 0.0)
    ybuf[0, _CONV_HALO:_CONV_HALO + tile, :] = glu(cur_ref[...])
    ybuf[0, _CONV_HALO + tile:2 * _CONV_HALO + tile, :] = jnp.where(i < last, glu(next_ref[...]), 0.0)
    span = tile + 2 * _CONV_HALO - V7X_SUBLANES
    for c in range(1, V7X_SUBLANES):
        ybuf[c, 0:span, :] = ybuf[0, c:c + span, :]

    pad = (CONV_TAPS - 1) // 2
    groups = _CONV_ROWS // V7X_SUBLANES

    def chunk(r, carry):
        r0 = pl.multiple_of(r * _CONV_ROWS, _CONV_ROWS)
        acc = [jnp.zeros((V7X_SUBLANES, CONV_WIDTH), F32)] * groups
        for k in range(CONV_TAPS):
            a, c = divmod(_CONV_HALO - pad + k, V7X_SUBLANES)
            wk = w_ref[k]
            for g in range(groups):
                rows = pl.ds(pl.multiple_of(r0 + (a + g) * V7X_SUBLANES, V7X_SUBLANES), V7X_SUBLANES)
                acc[g] = acc[g] + ybuf[c, rows, :] * wk
        for g in range(groups):
            cbuf[pl.ds(pl.multiple_of(r0 + g * V7X_SUBLANES, V7X_SUBLANES), V7X_SUBLANES), :] = acc[g]
        return carry

    lax.fori_loop(0, tile // _CONV_ROWS, chunk, 0)

    y = cbuf[...] + b_ref[...]
    mu = jnp.mean(y, axis=-1, keepdims=True)
    yc = y - mu
    var = jnp.mean(yc * yc, axis=-1, keepdims=True)
    z = yc * lax.rsqrt(var + NORM_EPS) * lg_ref[...] + lb_ref[...]
    o_ref[...] = _silu(z).astype(o_ref.dtype)


def conformer_conv(p_a, dw_w, dw_b, ln_g, ln_b):
    n = p_a.shape[0]
    tile = min(512, n)
    hb = tile // _CONV_HALO
    nhb = n // _CONV_HALO
    vec = pl.BlockSpec((1, CONV_WIDTH), lambda i: (0, 0))
    return pl.pallas_call(
        functools.partial(_conv_kernel, tile=tile),
        out_shape=jax.ShapeDtypeStruct((n, CONV_WIDTH), BF16),
        grid=(n // tile,),
        in_specs=[pl.BlockSpec((_CONV_HALO, IN_A), lambda i: (jnp.maximum(i * hb - 1, 0), 0)),
                  pl.BlockSpec((tile, IN_A), lambda i: (i, 0)),
                  pl.BlockSpec((_CONV_HALO, IN_A), lambda i: (jnp.minimum((i + 1) * hb, nhb - 1), 0)),
                  pl.BlockSpec((CONV_TAPS, V7X_SUBLANES, CONV_WIDTH), lambda i: (0, 0, 0)), vec, vec, vec],
        out_specs=pl.BlockSpec((tile, CONV_WIDTH), lambda i: (i, 0)),
        scratch_shapes=[pltpu.VMEM((V7X_SUBLANES, tile + 2 * _CONV_HALO, CONV_WIDTH), F32),
                        pltpu.VMEM((tile, CONV_WIDTH), F32)],
        compiler_params=_cparams(("parallel",), 4 * tile * IN_A * 4 + 9 * tile * CONV_WIDTH * 4 + (8 << 20)),
        name="conformer_conv",
    )(p_a, p_a, p_a, jnp.repeat(dw_w[:, None, :], V7X_SUBLANES, axis=1), dw_b, ln_g, ln_b)


def _rope_halves(t):
    return t[:, :QK_ROPE_DIM] + t[:, QK_ROPE_DIM:]


_NT = (((1,), (1,)), ((), ()))
_ATTN_QSCALE = (QK_DIM ** -0.5) * math.log2(math.e)
_VT_ROWS = V_HEAD_DIM + 16


def _qkv_kernel(cq_ref, ckv_ref, kr_ref, cs_ref, cst_ref, gq_ref, gkv_ref, wqt_ref, wk_ref, wvt_ref,
                qt_ref, k_ref, vt_ref):
    cqn = _rms(cq_ref[...], gq_ref[...]).astype(BF16)
    ckvn = _rms(ckv_ref[...], gkv_ref[...]).astype(BF16)
    cst = cst_ref[...]
    k_rope = _rope_halves(kr_ref[...] * cs_ref[...]).astype(k_ref.dtype)
    ones = jnp.ones((_VT_ROWS - V_HEAD_DIM, cqn.shape[0]), vt_ref.dtype)
    for h in range(MLA_HEADS):
        qa = lax.dot_general(wqt_ref[h], cqn, _NT, preferred_element_type=F32)
        qa = qa * _ATTN_QSCALE
        qt_ref[h, 0:QK_NOPE_DIM, :] = qa[:QK_NOPE_DIM].astype(qt_ref.dtype)
        t = qa[QK_NOPE_DIM:] * cst
        qt_ref[h, QK_NOPE_DIM:QK_DIM, :] = (t[:QK_ROPE_DIM] + t[QK_ROPE_DIM:]).astype(qt_ref.dtype)
        k_ref[h, :, 0:QK_NOPE_DIM] = jnp.dot(ckvn, wk_ref[h], preferred_element_type=F32).astype(k_ref.dtype)
        k_ref[h, :, QK_NOPE_DIM:QK_DIM] = k_rope
        vt_ref[h, 0:V_HEAD_DIM, :] = lax.dot_general(wvt_ref[h], ckvn, _NT,
                                                     preferred_element_type=F32).astype(vt_ref.dtype)
        vt_ref[h, V_HEAD_DIM:, :] = ones


def mla_qkv(c_q, c_kv, kr, cs, cst, g_q, g_kv, wqt, wk, wvt):
    n = c_q.shape[0]
    tm = min(512, n)
    row = lambda w: pl.BlockSpec((tm, w), lambda i: (i, 0))
    full = lambda a: pl.BlockSpec(a.shape, lambda i: (0,) * a.ndim)
    return pl.pallas_call(
        _qkv_kernel,
        out_shape=[jax.ShapeDtypeStruct((MLA_HEADS, QK_DIM, n), BF16),
                   jax.ShapeDtypeStruct((MLA_HEADS, n, QK_DIM), BF16),
                   jax.ShapeDtypeStruct((MLA_HEADS, _VT_ROWS, n), BF16)],
        grid=(n // tm,),
        in_specs=[row(Q_LORA_RANK), row(KV_LORA_RANK), row(2 * QK_ROPE_DIM), row(2 * QK_ROPE_DIM),
                  pl.BlockSpec((2 * QK_ROPE_DIM, tm), lambda i: (0, i)),
                  full(g_q), full(g_kv), full(wqt), full(wk), full(wvt)],
        out_specs=[pl.BlockSpec((MLA_HEADS, QK_DIM, tm), lambda i: (0, 0, i)),
                   pl.BlockSpec((MLA_HEADS, tm, QK_DIM), lambda i: (0, i, 0)),
                   pl.BlockSpec((MLA_HEADS, _VT_ROWS, tm), lambda i: (0, 0, i))],
        compiler_params=_cparams(("parallel",), 32 << 20),
        name="mla_qkv",
    )(c_q, c_kv, kr, cs, cst, g_q, g_kv, wqt, wk, wvt)


def _attn_kernel(*refs, chunks):
    n_src = len(chunks)
    qt_ref = refs[0]
    kv_refs = refs[1:1 + 2 * n_src]
    o_ref = refs[1 + 2 * n_src]
    m_sc, acc_sc, st0, st1 = refs[2 + 2 * n_src:]
    st_sc = (st0, st1)
    qt = qt_ref[...]
    m_sc[...] = jnp.full(m_sc.shape, -jnp.inf, F32)
    acc_sc[...] = jnp.zeros(acc_sc.shape, F32)

    def scores(kb):
        return jnp.dot(kb, qt, preferred_element_type=F32)

    def softmax_pv(st, vtb):
        m_prev = m_sc[...]
        m_new = jnp.maximum(m_prev, jnp.max(st, axis=0, keepdims=True))
        alpha = jnp.exp2(m_prev - m_new)
        p = jnp.exp2((st - m_new).astype(BF16))
        acc_sc[...] = alpha * acc_sc[...] + jnp.dot(vtb, p, preferred_element_type=F32)
        m_sc[...] = m_new

    for s_idx, (length, tk) in enumerate(chunks):
        k_ref, vt_ref = kv_refs[2 * s_idx], kv_refs[2 * s_idx + 1]
        if length == tk:
            softmax_pv(scores(k_ref[...]), vt_ref[...])
            continue
        steps = length // tk
        assert steps % 2 == 0

        def kblk(j, k_ref=k_ref, tk=tk):
            return k_ref[pl.ds(pl.multiple_of(j * tk, tk), tk), :]

        def vblk(j, vt_ref=vt_ref, tk=tk):
            return vt_ref[:, pl.ds(pl.multiple_of(j * tk, tk), tk)]

        st_sc[0][0:tk, :] = scores(kblk(0))

        def pair(jj, carry, tk=tk):
            j0 = 2 * jj
            st_sc[1][0:tk, :] = scores(kblk(j0 + 1))
            softmax_pv(st_sc[0][0:tk, :], vblk(j0))
            st_sc[0][0:tk, :] = scores(kblk(j0 + 2))
            softmax_pv(st_sc[1][0:tk, :], vblk(j0 + 1))
            return carry

        lax.fori_loop(0, steps // 2 - 1, pair, 0)
        st_sc[1][0:tk, :] = scores(kblk(steps - 1))
        softmax_pv(st_sc[0][0:tk, :], vblk(steps - 2))
        softmax_pv(st_sc[1][0:tk, :], vblk(steps - 1))

    acc = acc_sc[...]
    o = acc[:V_HEAD_DIM] / acc[V_HEAD_DIM:V_HEAD_DIM + 1]
    o_ref[...] = o.T.astype(o_ref.dtype)


_ATTN_TQ = 512
_ATTN_TK = 1024


def attention(qt, kv_sources):
    heads, _, lq = qt.shape
    tq = min(_ATTN_TQ, lq)
    chunks = []
    in_specs = [pl.BlockSpec((None, QK_DIM, tq), lambda h, i: (h, 0, i))]
    operands = [qt]
    kv_bytes = 0
    for k, vt in kv_sources:
        lk = k.shape[1]
        chunks.append((lk, min(_ATTN_TK, lk)))
        in_specs.append(pl.BlockSpec((None, lk, QK_DIM), lambda h, i: (h, 0, 0)))
        in_specs.append(pl.BlockSpec((None, _VT_ROWS, lk), lambda h, i: (h, 0, 0)))
        operands += [k, vt]
        kv_bytes += lk * (2 * V7X_LANES + _VT_ROWS) * 2
    return pl.pallas_call(
        functools.partial(_attn_kernel, chunks=tuple(chunks)),
        out_shape=jax.ShapeDtypeStruct((lq, heads * V_HEAD_DIM), BF16),
        grid=(heads, lq // tq),
        in_specs=in_specs,
        out_specs=pl.BlockSpec((tq, V_HEAD_DIM), lambda h, i: (i, h)),
        scratch_shapes=[pltpu.VMEM((1, tq), F32), pltpu.VMEM((_VT_ROWS, tq), F32),
                        pltpu.VMEM((_ATTN_TK, tq), F32), pltpu.VMEM((_ATTN_TK, tq), F32)],
        compiler_params=_cparams(("parallel", "arbitrary"), 2 * kv_bytes + (16 << 20)),
        name="attention",
    )(*operands)


_SHORT_HALO = 8


def _short_kernel(prev_ref, cur_ref, next_ref, w_ref, b_ref, x1_ref, x2_ref, v_ref, ubuf, *, tile):
    i = pl.program_id(0)
    last = pl.num_programs(0) - 1
    ubuf[0:_SHORT_HALO, :] = jnp.where(i > 0, prev_ref[...], 0.0)
    ubuf[_SHORT_HALO:_SHORT_HALO + tile, :] = cur_ref[...]
    ubuf[_SHORT_HALO + tile:2 * _SHORT_HALO + tile, :] = jnp.where(i < last, next_ref[...], 0.0)
    pad = (HYENA_SHORT_TAPS - 1) // 2
    w = w_ref[...]
    for o_ref, c0 in ((x1_ref, 0), (x2_ref, HYENA_WIDTH), (v_ref, 2 * HYENA_WIDTH)):
        acc = jnp.zeros((tile, HYENA_WIDTH), F32) + b_ref[:, c0:c0 + HYENA_WIDTH]
        for k in range(HYENA_SHORT_TAPS):
            acc = acc + (ubuf[_SHORT_HALO - pad + k:_SHORT_HALO - pad + k + tile, c0:c0 + HYENA_WIDTH]
                         * w[k:k + 1, c0:c0 + HYENA_WIDTH])
        o_ref[...] = acc


def hyena_short_conv(p_c, w, b):
    n = p_c.shape[0]
    tile = min(256, n)
    hb = tile // _SHORT_HALO
    nhb = n // _SHORT_HALO
    out = jax.ShapeDtypeStruct((n, HYENA_WIDTH), F32)
    o_spec = pl.BlockSpec((tile, HYENA_WIDTH), lambda i: (i, 0))
    return pl.pallas_call(
        functools.partial(_short_kernel, tile=tile),
        out_shape=[out, out, out],
        grid=(n // tile,),
        in_specs=[pl.BlockSpec((_SHORT_HALO, IN_C), lambda i: (jnp.maximum(i * hb - 1, 0), 0)),
                  pl.BlockSpec((tile, IN_C), lambda i: (i, 0)),
                  pl.BlockSpec((_SHORT_HALO, IN_C), lambda i: (jnp.minimum((i + 1) * hb, nhb - 1), 0)),
                  pl.BlockSpec((HYENA_SHORT_TAPS, IN_C), lambda i: (0, 0)),
                  pl.BlockSpec((1, IN_C), lambda i: (0, 0))],
        out_specs=[o_spec, o_spec, o_spec],
        scratch_shapes=[pltpu.VMEM((tile + 2 * _SHORT_HALO, IN_C), F32)],
        compiler_params=_cparams(("parallel",), 32 << 20),
        name="hyena_short_conv",
    )(p_c, p_c, p_c, w, b)


_FILT_W = HYENA_ORDER * HYENA_WIDTH
_EMB_PAD = 128


def _filter_kernel(emb_ref, w1_ref, b1_ref, f1_ref, w2_ref, b2_ref, f2_ref, w3_ref, dl_ref,
                   full_ref, l1_ref, *, tiles_per_dir):
    i = pl.program_id(0)
    emb = emb_ref[...]
    hid = jnp.sin(f1_ref[...] * (jnp.dot(emb, w1_ref[...], preferred_element_type=F32, precision=HIGHEST)
                                 + b1_ref[...]))
    hid = jnp.sin(f2_ref[...] * (jnp.dot(hid, w2_ref[...], preferred_element_type=F32, precision=HIGHEST)
                                 + b2_ref[...]))
    filt = _dot3(*_split(hid), w3_ref[...])
    decay = jnp.exp(-emb[:, 0:1] * dl_ref[...])
    rows = lax.broadcasted_iota(jnp.int32, (filt.shape[0], 1), 0)
    keep = jnp.logical_or(i != tiles_per_dir, rows != 0)
    for o in range(HYENA_ORDER):
        sl = slice(o * HYENA_WIDTH, (o + 1) * HYENA_WIDTH)
        full_ref[:, sl] = jnp.where(keep, filt[:, sl] * decay, 0.0)

    @pl.when(i == 0)
    def _():
        l1_ref[...] = jnp.zeros(l1_ref.shape, F32)

    l1_ref[0:1, :] += jnp.sum(jnp.abs(full_ref[...]), axis=0, keepdims=True)


def hyena_filters(n, w1, b1, f1, w2, b2, f2, w3):
    r = jnp.arange(2 * n, dtype=F32)[:, None]
    pos = jnp.where(r < n, r, 2 * n - r)
    t = pos / (n - 1)
    wpos = 2.0 * math.pi * pos / n
    f = jnp.linspace(1e-4, HYENA_BANDS - 1, HYENA_BANDS, dtype=F32)[None, :]
    emb_full = jnp.concatenate([t, jnp.cos(f * wpos), -jnp.sin(f * wpos),
                                jnp.zeros((2 * n, _EMB_PAD - HYENA_EMB_DIM), F32)], axis=-1)
    w1p = jnp.pad(w1, ((0, _EMB_PAD - HYENA_EMB_DIM), (0, 0)))
    w3d = w3.reshape(HYENA_FILTER_HIDDEN, 2, _FILT_W).transpose(1, 0, 2)
    min_decay = math.log(HYENA_DECAY_TARGET) / HYENA_SLOW_DECAY_PCT
    max_decay = math.log(HYENA_DECAY_TARGET) / HYENA_FAST_DECAY_PCT
    deltas = jnp.abs(jnp.linspace(min_decay, max_decay, HYENA_WIDTH, dtype=F32))[None, :]
    tile = min(512, n)
    tiles_per_dir = n // tile
    hv = pl.BlockSpec((1, HYENA_FILTER_HIDDEN), lambda i: (0, 0))
    return pl.pallas_call(
        functools.partial(_filter_kernel, tiles_per_dir=tiles_per_dir),
        out_shape=[jax.ShapeDtypeStruct((2 * n, _FILT_W), F32),
                   jax.ShapeDtypeStruct((V7X_SUBLANES, _FILT_W), F32)],
        grid=(2 * tiles_per_dir,),
        in_specs=[pl.BlockSpec((tile, _EMB_PAD), lambda i: (i, 0)),
                  pl.BlockSpec((_EMB_PAD, HYENA_FILTER_HIDDEN), lambda i: (0, 0)), hv, hv,
                  pl.BlockSpec((HYENA_FILTER_HIDDEN, HYENA_FILTER_HIDDEN), lambda i: (0, 0)), hv, hv,
                  pl.BlockSpec((None, HYENA_FILTER_HIDDEN, _FILT_W), lambda i: (i // tiles_per_dir, 0, 0)),
                  pl.BlockSpec((1, HYENA_WIDTH), lambda i: (0, 0))],
        out_specs=[pl.BlockSpec((tile, _FILT_W), lambda i: (i, 0)),
                   pl.BlockSpec((V7X_SUBLANES, _FILT_W), lambda i: (0, 0))],
        compiler_params=_cparams(("arbitrary",), 32 << 20),
        name="hyena_filters",
    )(emb_full, w1p, b1, f1, w2, b2, f2, w3d, deltas)


def _fft_tables(n1):
    n2 = FFT_N2
    nb = n1 // 2 + 1
    nbp = -(-nb // V7X_SUBLANES) * V7X_SUBLANES
    k1 = np.arange(nbp)[:, None].astype(np.float64)
    valid = (np.arange(nbp) < nb)[:, None]
    s1 = np.arange(n1)[None, :]
    ang = 2.0 * np.pi * k1 * s1 / n1
    f1 = np.concatenate([np.where(valid, np.cos(ang), 0.0), np.where(valid, -np.sin(ang), 0.0)], axis=0)
    idx = np.arange(n2)
    ang2 = 2.0 * np.pi * np.outer(idx, idx) / n2
    f2 = np.stack([np.cos(ang2), -np.sin(ang2)])
    angt = 2.0 * np.pi * k1 * idx[None, :] / (n1 * n2)
    tw_f = np.stack([np.cos(angt), -np.sin(angt)])[:, :, None, :]
    t1 = np.arange(n1 // 2)[:, None]
    ang1 = 2.0 * np.pi * t1 * np.arange(nbp)[None, :] / n1
    inv1 = np.stack([np.cos(ang1), np.sin(ang1)])
    wk = np.where((np.arange(nbp) == 0) | (np.arange(nbp) == n1 // 2), 1.0, 2.0) * (np.arange(nbp) < nb)
    angb = 2.0 * np.pi * idx[:, None] * np.arange(nbp)[None, :] / (n1 * n2)
    tw_i = np.stack([np.cos(angb) * wk, np.sin(angb) * wk], axis=1) / (n1 * n2)
    as32 = lambda a: jnp.asarray(a, dtype=F32)
    return dict(nb=nb, nbp=nbp, f1=as32(f1), f2=as32(f2), tw_f=as32(tw_f), inv1=as32(inv1), tw_i=as32(tw_i))


def _split(a):
    hi = a.astype(BF16)
    return hi, (a - hi.astype(F32)).astype(BF16)


def _dot3(a_hi, a_lo, b):
    m = a_hi.shape[0]
    b_hi, b_lo = _split(b)
    t = jnp.dot(jnp.concatenate([a_hi, a_lo], axis=0), b_hi, preferred_element_type=F32)
    return t[:m] + t[m:] + jnp.dot(a_hi, b_lo, preferred_element_type=F32)


_FFT_T2 = 8


def _fft1_kernel(x_ref, f_ref, o_ref):
    nbp = o_ref.shape[1]
    a = _dot3(*_split(f_ref[...]), x_ref[...])
    o_ref[0] = a[:nbp]
    o_ref[1] = a[nbp:]


def fft_level1(x, f1, n1):
    rows, ch = x.shape
    s1 = rows // FFT_N2
    cols = FFT_N2 * ch
    nbp = f1.shape[0] // 2
    tc = min(8192, cols)
    out = pl.pallas_call(
        _fft1_kernel,
        out_shape=jax.ShapeDtypeStruct((2, nbp, cols), F32),
        grid=(cols // tc,),
        in_specs=[pl.BlockSpec((s1, tc), lambda j: (0, j)),
                  pl.BlockSpec((2 * nbp, s1), lambda j: (0, 0))],
        out_specs=pl.BlockSpec((2, nbp, tc), lambda j: (0, 0, j)),
        compiler_params=_cparams(("parallel",), 3 * (s1 + 4 * nbp) * tc * 4 + (8 << 20)),
        name="fft_level1",
    )(x.reshape(s1, cols), f1[:, :s1])
    return out.reshape(2, nbp, FFT_N2, ch)


def _cdot3(g_hi, g_lo, br, bi):
    m = g_hi.shape[0] // 2
    t1 = _dot3(g_hi, g_lo, br)
    t2 = _dot3(g_hi, g_lo, bi)
    return t1[:m] - t2[m:], t2[:m] + t1[m:]


def _twiddled_dft(f_ref, t_ref):
    fr, fi = f_ref[0], f_ref[1]
    tr, ti = t_ref[0], t_ref[1]
    return _split(jnp.concatenate([fr * tr - fi * ti, fr * ti + fi * tr], axis=0))


def _fft2_spectrum_kernel(a_ref, f_ref, t_ref, l1_ref, o_ref, *, nb):
    k1 = pl.program_id(0)

    @pl.when(k1 < nb)
    def _():
        g_hi, g_lo = _twiddled_dft(f_ref, t_ref)
        xr, xi = _cdot3(g_hi, g_lo, a_ref[0], a_ref[1])
        inv = 1.0 / l1_ref[0:1, :]
        o_ref[0] = xr * inv
        o_ref[1] = xi * inv

    @pl.when(k1 >= nb)
    def _():
        o_ref[...] = jnp.zeros(o_ref.shape, F32)


def fft_level2_spectrum(a, tabs, l1):
    _, nbp, n2, ch = a.shape
    blk = pl.BlockSpec((2, None, n2, ch), lambda k: (0, k, 0, 0))
    return pl.pallas_call(
        functools.partial(_fft2_spectrum_kernel, nb=tabs["nb"]),
        out_shape=jax.ShapeDtypeStruct(a.shape, F32),
        grid=(nbp,),
        in_specs=[blk, pl.BlockSpec((2, n2, n2), lambda k: (0, 0, 0)),
                  pl.BlockSpec((2, None, 1, n2), lambda k: (0, k, 0, 0)),
                  pl.BlockSpec((V7X_SUBLANES, ch), lambda k: (0, 0))],
        out_specs=blk,
        compiler_params=_cparams(("parallel",), 8 * 2 * n2 * ch * 4 + (8 << 20)),
        name="fft_level2_spectrum",
    )(a, tabs["f2"], tabs["tw_f"], l1)


def _fft2_conv_kernel(a_ref, kf_ref, f_ref, t_ref, o_ref, *, nb):
    k1 = pl.program_id(0)

    @pl.when(k1 < nb)
    def _():
        g_hi, g_lo = _twiddled_dft(f_ref, t_ref)
        xr, xi = _cdot3(g_hi, g_lo, a_ref[0], a_ref[1])
        kr, ki = kf_ref[0], kf_ref[1]
        yr = xr * kr - xi * ki
        yi = xr * ki + xi * kr
        n2 = f_ref.shape[1]
        f_hi, f_lo = _split(f_ref[...].reshape(2 * n2, n2))
        u1 = _dot3(f_hi, f_lo, yr)
        u2 = _dot3(f_hi, f_lo, yi)
        o_ref[0] = u1[:n2] + u2[n2:]
        o_ref[1] = u2[:n2] - u1[n2:]

    @pl.when(k1 >= nb)
    def _():
        o_ref[...] = jnp.zeros(o_ref.shape, F32)


def fft_level2_conv(a, kf, order, tabs):
    _, nbp, n2, ch = a.shape
    blk = pl.BlockSpec((2, None, n2, ch), lambda k: (0, k, 0, 0))
    return pl.pallas_call(
        functools.partial(_fft2_conv_kernel, nb=tabs["nb"]),
        out_shape=jax.ShapeDtypeStruct(a.shape, F32),
        grid=(nbp,),
        in_specs=[blk, pl.BlockSpec((2, None, n2, ch), lambda k: (0, k, 0, order)),
                  pl.BlockSpec((2, n2, n2), lambda k: (0, 0, 0)),
                  pl.BlockSpec((2, None, 1, n2), lambda k: (0, k, 0, 0))],
        out_specs=blk,
        compiler_params=_cparams(("parallel",), 10 * 2 * n2 * ch * 4 + (8 << 20)),
        name="fft_level2_conv",
    )(a, kf, tabs["f2"], tabs["tw_f"])


_INV_T2 = 8


def _ifft1_gate_kernel(b_ref, inv_ref, tw_ref, z_ref, g_ref, bias_ref, o_ref):
    ca, sa = inv_ref[0], inv_ref[1]
    for j in range(_INV_T2):
        cb, sb = tw_ref[j, 0:1, :], tw_ref[j, 1:2, :]
        mc = ca * cb - sa * sb
        ms = -(sa * cb + ca * sb)
        y = (jnp.dot(mc, b_ref[0, :, j, :], preferred_element_type=F32, precision=HIGHEST)
             + jnp.dot(ms, b_ref[1, :, j, :], preferred_element_type=F32, precision=HIGHEST))
        z = z_ref[:, j, :]
        o_ref[:, j, :] = (g_ref[:, j, :] * (y + z * bias_ref[...])).astype(o_ref.dtype)


def ifft_level1_gate(b, z, gate, bias, tabs, out_dtype):
    _, nbp, n2, ch = b.shape
    n = z.shape[0]
    s1 = n // n2
    view = lambda a: a.reshape(s1, n2, ch)
    seq = pl.BlockSpec((s1, _INV_T2, ch), lambda j: (0, j, 0))
    out = pl.pallas_call(
        _ifft1_gate_kernel,
        out_shape=jax.ShapeDtypeStruct((s1, n2, ch), out_dtype),
        grid=(n2 // _INV_T2,),
        in_specs=[pl.BlockSpec((2, nbp, _INV_T2, ch), lambda j: (0, 0, j, 0)),
                  pl.BlockSpec((2, s1, nbp), lambda j: (0, 0, 0)),
                  pl.BlockSpec((_INV_T2, 2, nbp), lambda j: (j, 0, 0)),
                  seq, seq, pl.BlockSpec((1, ch), lambda j: (0, 0))],
        out_specs=seq,
        compiler_params=_cparams(("parallel",), 2 * (2 * nbp + 3 * s1) * _INV_T2 * ch * 4 + (8 << 20)),
        name="ifft_level1_gate",
    )(b, tabs["inv1"], tabs["tw_i"], view(z), view(gate), bias)
    return out.reshape(n, ch)


def hyena_long_conv_fft(x1, x2, v, full, l1, bias):
    n, ch = v.shape
    n1 = 2 * n // FFT_N2
    tabs = _fft_tables(n1)
    kf = fft_level2_spectrum(fft_level1(full, tabs["f1"], n1), tabs, l1)
    z = v
    for o, gate in enumerate((x1, x2)):
        a = fft_level1(z, tabs["f1"], n1)
        b = fft_level2_conv(a, kf, o, tabs)
        last = o == HYENA_ORDER - 1
        z = ifft_level1_gate(b, z, gate, bias[o:o + 1], tabs, BF16 if last else F32)
    return z


def _direct_conv_kernel(lo_ref, hi_ref, l1_ref, z_ref, g_ref, bias_ref, o_ref, ebuf, *, n):
    inv = 1.0 / l1_ref[0:1, :]
    ebuf[0:n, :] = hi_ref[...] * inv
    ebuf[n:2 * n, :] = lo_ref[...] * inv
    for c0 in range(0, HYENA_WIDTH, V7X_LANES):
        cs = slice(c0, c0 + V7X_LANES)

        def body(g, acc, cs=cs):
            s0 = pl.multiple_of(g * V7X_SUBLANES, V7X_SUBLANES)
            base = pl.multiple_of(n - V7X_SUBLANES - s0, V7X_SUBLANES)
            win = ebuf[pl.ds(base, n + V7X_SUBLANES), cs]
            zb = z_ref[pl.ds(s0, V7X_SUBLANES), cs]
            for j in range(V7X_SUBLANES):
                acc = acc + win[V7X_SUBLANES - j:V7X_SUBLANES - j + n, :] * zb[j:j + 1, :]
            return acc

        y = lax.fori_loop(0, n // V7X_SUBLANES, body, jnp.zeros((n, V7X_LANES), F32))
        o_ref[:, cs] = (g_ref[:, cs] * (y + z_ref[:, cs] * bias_ref[:, cs])).astype(o_ref.dtype)


def hyena_long_conv_direct(x1, x2, v, full, l1, bias):
    n, ch = v.shape
    z = v
    for o, gate in enumerate((x1, x2)):
        last = o == HYENA_ORDER - 1
        seq = pl.BlockSpec((n, ch), lambda i: (0, 0))
        z = pl.pallas_call(
            functools.partial(_direct_conv_kernel, n=n),
            out_shape=jax.ShapeDtypeStruct((n, ch), BF16 if last else F32),
            grid=(1,),
            in_specs=[pl.BlockSpec((n, ch), lambda i, o=o: (0, o)),
                      pl.BlockSpec((n, ch), lambda i, o=o: (1, o)),
                      pl.BlockSpec((V7X_SUBLANES, ch), lambda i, o=o: (0, o)),
                      seq, seq, pl.BlockSpec((1, ch), lambda i: (0, 0))],
            out_specs=seq,
            scratch_shapes=[pltpu.VMEM((2 * n, ch), F32)],
            compiler_params=_cparams(("arbitrary",), 32 << 20),
            name="hyena_direct_conv",
        )(full, full, l1, z, gate, bias[o:o + 1])
    return z


_DIRECT_CONV_MAX = 512


def hyena_mix(p_c, short_w, short_b, w1, b1, f1, w2, b2, f2, w3, bias):
    n = p_c.shape[0]
    x1, x2, v = hyena_short_conv(p_c, short_w, short_b)
    full, l1 = hyena_filters(n, w1, b1, f1, w2, b2, f2, w3)
    if n <= _DIRECT_CONV_MAX:
        return hyena_long_conv_direct(x1, x2, v, full, l1, bias)
    return hyena_long_conv_fft(x1, x2, v, full, l1, bias)


def _out_kernel(cv_ref, at_ref, hy_ref, w_ref, x_ref, g_ref, gate_ref, o_ref):
    y = jnp.dot(cv_ref[...], w_ref[0:CONV_WIDTH, :], preferred_element_type=F32)
    y = y + jnp.dot(at_ref[...], w_ref[CONV_WIDTH:CONV_WIDTH + MLA_WIDTH, :], preferred_element_type=F32)
    y = y + jnp.dot(hy_ref[...], w_ref[CONV_WIDTH + MLA_WIDTH:, :], preferred_element_type=F32)
    o_ref[...] = x_ref[...] + gate_ref[...] * _rms(y, g_ref[...])


def out_projection(conv_o, attn, hy_o, w_out_bf16, x, g, gate):
    n, d = x.shape
    tm = min(512, n)
    mix = w_out_bf16.shape[0]
    vec = pl.BlockSpec((1, d), lambda i: (0, 0))
    row = lambda w: pl.BlockSpec((tm, w), lambda i: (i, 0))
    return pl.pallas_call(
        _out_kernel,
        out_shape=jax.ShapeDtypeStruct((n, d), F32),
        grid=(n // tm,),
        in_specs=[row(CONV_WIDTH), row(MLA_WIDTH), row(HYENA_WIDTH),
                  _resident((mix, d), lambda i: (0, 0)), row(d), vec, vec],
        out_specs=row(d),
        compiler_params=_cparams(("parallel",), mix * d * 2 + 6 * tm * d * 4 + (8 << 20)),
        name="out_projection",
    )(conv_o, attn, hy_o, w_out_bf16, x, g, gate)


def _ffn_kernel(x_ref, gpre_ref, sh_ref, sc_ref, w1_ref, w2_ref, gpost_ref, gate_ref, o_ref, h_sc):
    f = pl.program_id(1)

    @pl.when(f == 0)
    def _():
        h = _rms(x_ref[...], gpre_ref[...]) * (1.0 + sc_ref[...]) + sh_ref[...]
        h_sc[...] = h.astype(BF16)
        o_ref[...] = jnp.zeros(o_ref.shape, F32)

    a = jnp.dot(h_sc[...], w1_ref[...], preferred_element_type=F32)
    a = jnp.square(jnp.maximum(a, 0.0)).astype(BF16)
    o_ref[...] += jnp.dot(a, w2_ref[...], preferred_element_type=F32)

    @pl.when(f == pl.num_programs(1) - 1)
    def _():
        o_ref[...] = x_ref[...] + gate_ref[...] * _rms(o_ref[...], gpost_ref[...])


def ffn(x, g_pre, shift, scale, w1_bf16, w2_bf16, g_post, gate):
    n, d = x.shape
    dff = w1_bf16.shape[1]
    tm = min(512, n)
    tf = 1024
    vec = pl.BlockSpec((1, d), lambda i, f: (0, 0))
    row = pl.BlockSpec((tm, d), lambda i, f: (i, 0))
    vmem = 4 * tm * d * 4 + tm * d * 2 + 4 * d * tf * 2 + 3 * tm * tf * 4 + (8 << 20)
    return pl.pallas_call(
        _ffn_kernel,
        out_shape=jax.ShapeDtypeStruct((n, d), F32),
        grid=(n // tm, dff // tf),
        in_specs=[row, vec, vec, vec,
                  pl.BlockSpec((d, tf), lambda i, f: (0, f)),
                  pl.BlockSpec((tf, d), lambda i, f: (f, 0)), vec, vec],
        out_specs=row,
        scratch_shapes=[pltpu.VMEM((tm, d), BF16)],
        compiler_params=_cparams(("parallel", "arbitrary"), vmem),
        name="ffn",
    )(x, g_pre, shift, scale, w1_bf16, w2_bf16, g_post, gate)


def _rot_cols(w):
    half = w.shape[-1] // 2
    return jnp.concatenate([-w[..., half:], w[..., :half]], axis=-1)


def _rope_table(rows):
    row = jnp.repeat(jnp.arange(rows, dtype=F32), GRID_W)
    col = jnp.tile(jnp.arange(GRID_W, dtype=F32), rows)
    axis_dim = QK_ROPE_DIM // 2
    inv = 1.0 / (ROPE_BASE ** (jnp.arange(0, axis_dim, 2, dtype=F32) / axis_dim))
    ang = jnp.concatenate([row[:, None] * inv, col[:, None] * inv], axis=-1)
    cos, sin = jnp.cos(ang), jnp.sin(ang)
    return jnp.concatenate([cos, cos, sin, sin], axis=-1)


def kernel(x, c, ctx, c_ctx, w_mod, b_mod, g_pre_mix, g_post_mix, g_pre_ffn, g_post_ffn, w_in, conv_dw_w, conv_dw_b, conv_ln_g, conv_ln_b, mla_q_norm, mla_w_uq, mla_kv_norm, mla_w_ukv, hy_short_w, hy_short_b, hy_w1, hy_b1, hy_freq1, hy_w2, hy_b2, hy_freq2, hy_w3, hy_bias, w_out, w_ff1, w_ff2):
    batch, seq, d = x.shape
    assert batch == 1 and c.shape[0] == 1 and ctx.shape[0] == 1
    depth = w_mod.shape[0]
    ctx_len = ctx.shape[1]
    xs, xc = x[0], ctx[0]

    mod = modulation(jnp.stack([c[0], c_ctx]), w_mod, b_mod)

    cs_lat = _rope_table(seq // GRID_W)
    cs_ctx = jnp.concatenate([jnp.ones((ctx_len, QK_ROPE_DIM), F32), jnp.zeros((ctx_len, QK_ROPE_DIM), F32)], -1)
    row = lambda a: a.reshape(1, -1)
    qa, qb_end = Q_LORA_RANK, Q_LORA_RANK + KV_LORA_RANK

    for l in range(depth):
        last = l == depth - 1
        mvec = lambda r, j: mod[l, r:r + 1, j * d:(j + 1) * d]
        wl = w_in[l]
        w_b = wl[:, IN_A:IN_A + IN_B]
        w_kr = w_b[:, qb_end:]
        w_in_p = jnp.concatenate([wl[:, :IN_A], wl[:, IN_A + IN_B:], w_b[:, :qb_end], w_kr, _rot_cols(w_kr)],
                                 axis=-1).astype(BF16)
        wq = mla_w_uq[l].reshape(Q_LORA_RANK, MLA_HEADS, QK_DIM).transpose(1, 0, 2)
        wq = jnp.concatenate([wq, _rot_cols(wq[..., QK_NOPE_DIM:])], axis=-1)
        wqt = wq.transpose(0, 2, 1).astype(BF16)
        wkv = mla_w_ukv[l].reshape(KV_LORA_RANK, MLA_HEADS, QK_NOPE_DIM + V_HEAD_DIM).transpose(1, 0, 2)
        wk = wkv[..., :QK_NOPE_DIM].astype(BF16)
        wvt = wkv[..., QK_NOPE_DIM:].transpose(0, 2, 1).astype(BF16)
        w_out_b = w_out[l].astype(BF16)
        w1_b, w2_b = w_ff1[l].astype(BF16), w_ff2[l].astype(BF16)

        def project(stream, r, cs):
            p_a, p_c, c_q, c_kv, kr = in_projection(stream, row(g_pre_mix[l]), mvec(r, 0), mvec(r, 1), w_in_p)
            qt, k, vt = mla_qkv(c_q, c_kv, kr, cs, cs.T, row(mla_q_norm[l]), row(mla_kv_norm[l]), wqt, wk, wvt)
            return p_a, p_c, qt, k, vt

        def finish(stream, r, p_a, p_c, attn):
            conv_o = conformer_conv(p_a, conv_dw_w[l], row(conv_dw_b[l]), row(conv_ln_g[l]), row(conv_ln_b[l]))
            hy_o = hyena_mix(p_c, hy_short_w[l], row(hy_short_b[l]), hy_w1[l], row(hy_b1[l]), row(hy_freq1[l]),
                             hy_w2[l], row(hy_b2[l]), row(hy_freq2[l]), hy_w3[l], hy_bias[l])
            y = out_projection(conv_o, attn, hy_o, w_out_b, stream, row(g_post_mix[l]), mvec(r, 2))
            return ffn(y, row(g_pre_ffn[l]), mvec(r, 3), mvec(r, 4), w1_b, w2_b, row(g_post_ffn[l]), mvec(r, 5))

        pc_a, pc_c, qc, kc, vc = project(xc, 1, cs_ctx)
        p_a, p_c, q, k, v = project(xs, 0, cs_lat)
        attn = attention(q, [(kc, vc), (k, v)])
        new_x = finish(xs, 0, p_a, p_c, attn)
        if not last:
            xc = finish(xc, 1, pc_a, pc_c, attention(qc, [(kc, vc)]))
        xs = new_x
    return xs[None]
```

```python
import functools
import math

import numpy as np
import jax
import jax.numpy as jnp
from jax import lax
from jax.experimental import pallas as pl
from jax.experimental.pallas import tpu as pltpu

GRID_W = 64
CONV_WIDTH = 512
CONV_TAPS = 31
MLA_HEADS = 8
QK_NOPE_DIM = 128
QK_ROPE_DIM = 64
V_HEAD_DIM = 128
Q_LORA_RANK = 512
KV_LORA_RANK = 256
MLA_WIDTH = MLA_HEADS * V_HEAD_DIM
HYENA_WIDTH = 512
HYENA_ORDER = 2
HYENA_SHORT_TAPS = 3
HYENA_EMB_DIM = 33
HYENA_BANDS = (HYENA_EMB_DIM - 1) // 2
HYENA_FILTER_HIDDEN = 64
HYENA_FAST_DECAY_PCT = 0.3
HYENA_SLOW_DECAY_PCT = 1.5
HYENA_DECAY_TARGET = 1e-2
N_MOD = 6
ROPE_BASE = 10000.0
NORM_EPS = 1e-6
IN_A = 2 * CONV_WIDTH
IN_B = Q_LORA_RANK + KV_LORA_RANK + QK_ROPE_DIM
IN_C = (HYENA_ORDER + 1) * HYENA_WIDTH
QK_DIM = QK_NOPE_DIM + QK_ROPE_DIM

V7X_LANES = 128
V7X_SUBLANES = 8
V7X_VMEM_BYTES = 64 * 1024 * 1024
V7X_VMEM_BUDGET = 56 * 1024 * 1024

FFT_N2 = 256

F32 = jnp.float32
BF16 = jnp.bfloat16
HIGHEST = lax.Precision.HIGHEST


def _cparams(semantics, vmem_bytes):
    return pltpu.CompilerParams(dimension_semantics=semantics,
                                vmem_limit_bytes=int(min(max(vmem_bytes, 16 << 20), V7X_VMEM_BUDGET)))


def _resident(block_shape, index_map):
    return pl.BlockSpec(block_shape, index_map, pipeline_mode=pl.Buffered(1))


def _rms(x, g):
    ms = jnp.mean(x * x, axis=-1, keepdims=True)
    return x * lax.rsqrt(ms + NORM_EPS) * g


def _silu(x):
    return x * jax.nn.sigmoid(x)


_MOD_VECS = 2


def _mod_kernel(cb_ref, w_ref, b_ref, o_ref, s_sc):
    d, tn = w_ref.shape
    reps = tn // V7X_LANES

    @pl.when(jnp.logical_and(pl.program_id(0) == 0, pl.program_id(1) == 0))
    def _():
        s_sc[...] = _silu(cb_ref[...])

    def body(kb, accs):
        k0 = pl.multiple_of(kb * V7X_SUBLANES, V7X_SUBLANES)
        wblk = w_ref[pl.ds(k0, V7X_SUBLANES), :]
        out = []
        for r in range(_MOD_VECS):
            s = s_sc[r, pl.ds(k0, V7X_SUBLANES), :]
            out.append(accs[r] + wblk * jnp.concatenate([s] * reps, axis=1))
        return tuple(out)

    zero = jnp.zeros((V7X_SUBLANES, tn), F32)
    accs = lax.fori_loop(0, d // V7X_SUBLANES, body, (zero,) * _MOD_VECS)
    o_ref[...] = jnp.zeros(o_ref.shape, F32)
    for r in range(_MOD_VECS):
        o_ref[r:r + 1, :] = jnp.sum(accs[r], axis=0, keepdims=True) + b_ref[...]


def modulation(cvecs, w_mod, b_mod):
    n_layers, d, width = w_mod.shape
    tn = 2048
    cb = jnp.broadcast_to(cvecs[:, :, None], (_MOD_VECS, d, V7X_LANES))
    return pl.pallas_call(
        _mod_kernel,
        out_shape=jax.ShapeDtypeStruct((n_layers, V7X_SUBLANES, width), F32),
        grid=(n_layers, width // tn),
        in_specs=[pl.BlockSpec((_MOD_VECS, d, V7X_LANES), lambda l, j: (0, 0, 0)),
                  pl.BlockSpec((None, d, tn), lambda l, j: (l, 0, j)),
                  pl.BlockSpec((None, 1, tn), lambda l, j: (l, 0, j))],
        out_specs=pl.BlockSpec((None, V7X_SUBLANES, tn), lambda l, j: (l, 0, j)),
        scratch_shapes=[pltpu.VMEM((_MOD_VECS, d, V7X_LANES), F32)],
        compiler_params=_cparams(("arbitrary", "arbitrary"), 2 * d * tn * 4 + (8 << 20)),
        name="modulation",
    )(cb, w_mod, b_mod.reshape(n_layers, 1, width))


_IN_SPLITS = (IN_A, IN_C, Q_LORA_RANK, KV_LORA_RANK, 2 * QK_ROPE_DIM)
_IN_WIDTH = sum(_IN_SPLITS)


def _in_kernel(x_ref, g_ref, sh_ref, sc_ref, w_ref, *o_refs):
    h = _rms(x_ref[...], g_ref[...]) * (1.0 + sc_ref[...]) + sh_ref[...]
    hb = h.astype(BF16)
    off = 0
    for o_ref, width in zip(o_refs, _IN_SPLITS):
        o_ref[...] = jnp.dot(hb, w_ref[:, off:off + width], preferred_element_type=F32)
        off += width


def in_projection(x, g, shift, scale, w_in_bf16):
    n, d = x.shape
    tm = min(512, n)
    vec = pl.BlockSpec((1, d), lambda i: (0, 0))
    vmem = 2 * tm * d * 4 + d * _IN_WIDTH * 2 + 3 * tm * _IN_WIDTH * 4 + (4 << 20)
    return pl.pallas_call(
        _in_kernel,
        out_shape=[jax.ShapeDtypeStruct((n, w), F32) for w in _IN_SPLITS],
        grid=(n // tm,),
        in_specs=[pl.BlockSpec((tm, d), lambda i: (i, 0)), vec, vec, vec,
                  _resident((d, _IN_WIDTH), lambda i: (0, 0))],
        out_specs=[pl.BlockSpec((tm, w), lambda i: (i, 0)) for w in _IN_SPLITS],
        compiler_params=_cparams(("parallel",), vmem),
        name="in_projection",
    )(x, g, shift, scale, w_in_bf16)


_CONV_HALO = 16
_CONV_ROWS = 32


def _conv_kernel(prev_ref, cur_ref, next_ref, w_ref, b_ref, lg_ref, lb_ref, o_ref, ybuf, cbuf, *, tile):
    i = pl.program_id(0)
    last = pl.num_programs(0) - 1

    def glu(v):
        return v[:, :CONV_WIDTH] * jax.nn.sigmoid(v[:, CONV_WIDTH:])

    ybuf[0, 0:_CONV_HALO, :] = jnp.where(i > 0, glu(prev_ref[...]), 0.0)
    ybuf[0, _CONV_HALO:_CONV_HALO + tile, :] = glu(cur_ref[...])
    ybuf[0, _CONV_HALO + tile:2 * _CONV_HALO + tile, :] = jnp.where(i < last, glu(next_ref[...]), 0.0)
    span = tile + 2 * _CONV_HALO - V7X_SUBLANES
    for c in range(1, V7X_SUBLANES):
        ybuf[c, 0:span, :] = ybuf[0, c:c + span, :]

    pad = (CONV_TAPS - 1) // 2
    groups = _CONV_ROWS // V7X_SUBLANES

    def chunk(r, carry):
        r0 = pl.multiple_of(r * _CONV_ROWS, _CONV_ROWS)
        acc = [jnp.zeros((V7X_SUBLANES, CONV_WIDTH), F32)] * groups
        for k in range(CONV_TAPS):
            a, c = divmod(_CONV_HALO - pad + k, V7X_SUBLANES)
            wk = w_ref[k]
            for g in range(groups):
                rows = pl.ds(pl.multiple_of(r0 + (a + g) * V7X_SUBLANES, V7X_SUBLANES), V7X_SUBLANES)
                acc[g] = acc[g] + ybuf[c, rows, :] * wk
        for g in range(groups):
            cbuf[pl.ds(pl.multiple_of(r0 + g * V7X_SUBLANES, V7X_SUBLANES), V7X_SUBLANES), :] = acc[g]
        return carry

    lax.fori_loop(0, tile // _CONV_ROWS, chunk, 0)

    y = cbuf[...] + b_ref[...]
    mu = jnp.mean(y, axis=-1, keepdims=True)
    yc = y - mu
    var = jnp.mean(yc * yc, axis=-1, keepdims=True)
    z = yc * lax.rsqrt(var + NORM_EPS) * lg_ref[...] + lb_ref[...]
    o_ref[...] = _silu(z).astype(o_ref.dtype)


def conformer_conv(p_a, dw_w, dw_b, ln_g, ln_b):
    n = p_a.shape[0]
    tile = min(512, n)
    hb = tile // _CONV_HALO
    nhb = n // _CONV_HALO
    vec = pl.BlockSpec((1, CONV_WIDTH), lambda i: (0, 0))
    return pl.pallas_call(
        functools.partial(_conv_kernel, tile=tile),
        out_shape=jax.ShapeDtypeStruct((n, CONV_WIDTH), BF16),
        grid=(n // tile,),
        in_specs=[pl.BlockSpec((_CONV_HALO, IN_A), lambda i: (jnp.maximum(i * hb - 1, 0), 0)),
                  pl.BlockSpec((tile, IN_A), lambda i: (i, 0)),
                  pl.BlockSpec((_CONV_HALO, IN_A), lambda i: (jnp.minimum((i + 1) * hb, nhb - 1), 0)),
                  pl.BlockSpec((CONV_TAPS, V7X_SUBLANES, CONV_WIDTH), lambda i: (0, 0, 0)), vec, vec, vec],
        out_specs=pl.BlockSpec((tile, CONV_WIDTH), lambda i: (i, 0)),
        scratch_shapes=[pltpu.VMEM((V7X_SUBLANES, tile + 2 * _CONV_HALO, CONV_WIDTH), F32),
                        pltpu.VMEM((tile, CONV_WIDTH), F32)],
        compiler_params=_cparams(("parallel",), 4 * tile * IN_A * 4 + 9 * tile * CONV_WIDTH * 4 + (8 << 20)),
        name="conformer_conv",
    )(p_a, p_a, p_a, jnp.repeat(dw_w[:, None, :], V7X_SUBLANES, axis=1), dw_b, ln_g, ln_b)


def _rope_halves(t):
    return t[:, :QK_ROPE_DIM] + t[:, QK_ROPE_DIM:]


_NT = (((1,), (1,)), ((), ()))
_ATTN_QSCALE = (QK_DIM ** -0.5) * math.log2(math.e)
_VT_ROWS = V_HEAD_DIM + 16


def _qkv_kernel(cq_ref, ckv_ref, kr_ref, cs_ref, cst_ref, gq_ref, gkv_ref, wqt_ref, wk_ref, wvt_ref,
                qt_ref, k_ref, vt_ref):
    cqn = _rms(cq_ref[...], gq_ref[...]).astype(BF16)
    ckvn = _rms(ckv_ref[...], gkv_ref[...]).astype(BF16)
    cst = cst_ref[...]
    k_rope = _rope_halves(kr_ref[...] * cs_ref[...]).astype(k_ref.dtype)
    ones = jnp.ones((_VT_ROWS - V_HEAD_DIM, cqn.shape[0]), vt_ref.dtype)
    for h in range(MLA_HEADS):
        qa = lax.dot_general(wqt_ref[h], cqn, _NT, preferred_element_type=F32)
        qa = qa * _ATTN_QSCALE
        qt_ref[h, 0:QK_NOPE_DIM, :] = qa[:QK_NOPE_DIM].astype(qt_ref.dtype)
        t = qa[QK_NOPE_DIM:] * cst
        qt_ref[h, QK_NOPE_DIM:QK_DIM, :] = (t[:QK_ROPE_DIM] + t[QK_ROPE_DIM:]).astype(qt_ref.dtype)
        k_ref[h, :, 0:QK_NOPE_DIM] = jnp.dot(ckvn, wk_ref[h], preferred_element_type=F32).astype(k_ref.dtype)
        k_ref[h, :, QK_NOPE_DIM:QK_DIM] = k_rope
        vt_ref[h, 0:V_HEAD_DIM, :] = lax.dot_general(wvt_ref[h], ckvn, _NT,
                                                     preferred_element_type=F32).astype(vt_ref.dtype)
        vt_ref[h, V_HEAD_DIM:, :] = ones


def mla_qkv(c_q, c_kv, kr, cs, cst, g_q, g_kv, wqt, wk, wvt):
    n = c_q.shape[0]
    tm = min(512, n)
    row = lambda w: pl.BlockSpec((tm, w), lambda i: (i, 0))
    full = lambda a: pl.BlockSpec(a.shape, lambda i: (0,) * a.ndim)
    return pl.pallas_call(
        _qkv_kernel,
        out_shape=[jax.ShapeDtypeStruct((MLA_HEADS, QK_DIM, n), BF16),
                   jax.ShapeDtypeStruct((MLA_HEADS, n, QK_DIM), BF16),
                   jax.ShapeDtypeStruct((MLA_HEADS, _VT_ROWS, n), BF16)],
        grid=(n // tm,),
        in_specs=[row(Q_LORA_RANK), row(KV_LORA_RANK), row(2 * QK_ROPE_DIM), row(2 * QK_ROPE_DIM),
                  pl.BlockSpec((2 * QK_ROPE_DIM, tm), lambda i: (0, i)),
                  full(g_q), full(g_kv), full(wqt), full(wk), full(wvt)],
        out_specs=[pl.BlockSpec((MLA_HEADS, QK_DIM, tm), lambda i: (0, 0, i)),
                   pl.BlockSpec((MLA_HEADS, tm, QK_DIM), lambda i: (0, i, 0)),
                   pl.BlockSpec((MLA_HEADS, _VT_ROWS, tm), lambda i: (0, 0, i))],
        compiler_params=_cparams(("parallel",), 32 << 20),
        name="mla_qkv",
    )(c_q, c_kv, kr, cs, cst, g_q, g_kv, wqt, wk, wvt)


def _attn_kernel(*refs, chunks):
    n_src = len(chunks)
    qt_ref = refs[0]
    kv_refs = refs[1:1 + 2 * n_src]
    o_ref = refs[1 + 2 * n_src]
    m_sc, acc_sc, st0, st1, mx0, mx1 = refs[2 + 2 * n_src:]
    st_sc, mx_sc = (st0, st1), (mx0, mx1)
    qt = qt_ref[...]
    m_sc[...] = jnp.full(m_sc.shape, -jnp.inf, F32)
    acc_sc[...] = jnp.zeros(acc_sc.shape, F32)

    def scores(kb, slot):
        tk = kb.shape[0]
        st = jnp.dot(kb, qt, preferred_element_type=F32)
        st_sc[slot][0:tk, :] = st
        mx_sc[slot][...] = jnp.max(st, axis=0, keepdims=True)

    def softmax_pv(slot, vtb):
        tk = vtb.shape[1]
        m_prev = m_sc[...]
        m_new = jnp.maximum(m_prev, mx_sc[slot][...])
        alpha = jnp.exp2(m_prev - m_new)
        p = jnp.exp2((st_sc[slot][0:tk, :] - m_new).astype(BF16))
        acc_sc[...] = alpha * acc_sc[...] + jnp.dot(vtb, p, preferred_element_type=F32)
        m_sc[...] = m_new

    for s_idx, (length, tk) in enumerate(chunks):
        k_ref, vt_ref = kv_refs[2 * s_idx], kv_refs[2 * s_idx + 1]
        if length == tk:
            scores(k_ref[...], 0)
            softmax_pv(0, vt_ref[...])
            continue
        steps = length // tk
        assert steps % 2 == 0

        def kblk(j, k_ref=k_ref, tk=tk):
            return k_ref[pl.ds(pl.multiple_of(j * tk, tk), tk), :]

        def vblk(j, vt_ref=vt_ref, tk=tk):
            return vt_ref[:, pl.ds(pl.multiple_of(j * tk, tk), tk)]

        scores(kblk(0), 0)

        def pair(jj, carry):
            j0 = 2 * jj
            scores(kblk(j0 + 1), 1)
            softmax_pv(0, vblk(j0))
            scores(kblk(j0 + 2), 0)
            softmax_pv(1, vblk(j0 + 1))
            return carry

        lax.fori_loop(0, steps // 2 - 1, pair, 0)
        scores(kblk(steps - 1), 1)
        softmax_pv(0, vblk(steps - 2))
        softmax_pv(1, vblk(steps - 1))

    acc = acc_sc[...]
    o = acc[:V_HEAD_DIM] / acc[V_HEAD_DIM:V_HEAD_DIM + 1]
    o_ref[...] = o.T.astype(o_ref.dtype)


_ATTN_TQ = 512
_ATTN_TK = 1024


def attention(qt, kv_sources):
    heads, _, lq = qt.shape
    tq = min(_ATTN_TQ, lq)
    chunks = []
    in_specs = [pl.BlockSpec((None, QK_DIM, tq), lambda h, i: (h, 0, i))]
    operands = [qt]
    kv_bytes = 0
    for k, vt in kv_sources:
        lk = k.shape[1]
        chunks.append((lk, min(_ATTN_TK, lk)))
        in_specs.append(pl.BlockSpec((None, lk, QK_DIM), lambda h, i: (h, 0, 0)))
        in_specs.append(pl.BlockSpec((None, _VT_ROWS, lk), lambda h, i: (h, 0, 0)))
        operands += [k, vt]
        kv_bytes += lk * (2 * V7X_LANES + _VT_ROWS) * 2
    return pl.pallas_call(
        functools.partial(_attn_kernel, chunks=tuple(chunks)),
        out_shape=jax.ShapeDtypeStruct((lq, heads * V_HEAD_DIM), BF16),
        grid=(heads, lq // tq),
        in_specs=in_specs,
        out_specs=pl.BlockSpec((tq, V_HEAD_DIM), lambda h, i: (i, h)),
        scratch_shapes=[pltpu.VMEM((1, tq), F32), pltpu.VMEM((_VT_ROWS, tq), F32),
                        pltpu.VMEM((_ATTN_TK, tq), F32), pltpu.VMEM((_ATTN_TK, tq), F32),
                        pltpu.VMEM((1, tq), F32), pltpu.VMEM((1, tq), F32)],
        compiler_params=_cparams(("parallel", "arbitrary"), 2 * kv_bytes + (16 << 20)),
        name="attention",
    )(*operands)


_SHORT_HALO = 8


def _short_kernel(prev_ref, cur_ref, next_ref, w_ref, b_ref, x1_ref, x2_ref, v_ref, ubuf, *, tile):
    i = pl.program_id(0)
    last = pl.num_programs(0) - 1
    ubuf[0:_SHORT_HALO, :] = jnp.where(i > 0, prev_ref[...], 0.0)
    ubuf[_SHORT_HALO:_SHORT_HALO + tile, :] = cur_ref[...]
    ubuf[_SHORT_HALO + tile:2 * _SHORT_HALO + tile, :] = jnp.where(i < last, next_ref[...], 0.0)
    pad = (HYENA_SHORT_TAPS - 1) // 2
    w = w_ref[...]
    for o_ref, c0 in ((x1_ref, 0), (x2_ref, HYENA_WIDTH), (v_ref, 2 * HYENA_WIDTH)):
        acc = jnp.zeros((tile, HYENA_WIDTH), F32) + b_ref[:, c0:c0 + HYENA_WIDTH]
        for k in range(HYENA_SHORT_TAPS):
            acc = acc + (ubuf[_SHORT_HALO - pad + k:_SHORT_HALO - pad + k + tile, c0:c0 + HYENA_WIDTH]
                         * w[k:k + 1, c0:c0 + HYENA_WIDTH])
        o_ref[...] = acc


def hyena_short_conv(p_c, w, b):
    n = p_c.shape[0]
    tile = min(256, n)
    hb = tile // _SHORT_HALO
    nhb = n // _SHORT_HALO
    out = jax.ShapeDtypeStruct((n, HYENA_WIDTH), F32)
    o_spec = pl.BlockSpec((tile, HYENA_WIDTH), lambda i: (i, 0))
    return pl.pallas_call(
        functools.partial(_short_kernel, tile=tile),
        out_shape=[out, out, out],
        grid=(n // tile,),
        in_specs=[pl.BlockSpec((_SHORT_HALO, IN_C), lambda i: (jnp.maximum(i * hb - 1, 0), 0)),
                  pl.BlockSpec((tile, IN_C), lambda i: (i, 0)),
                  pl.BlockSpec((_SHORT_HALO, IN_C), lambda i: (jnp.minimum((i + 1) * hb, nhb - 1), 0)),
                  pl.BlockSpec((HYENA_SHORT_TAPS, IN_C), lambda i: (0, 0)),
                  pl.BlockSpec((1, IN_C), lambda i: (0, 0))],
        out_specs=[o_spec, o_spec, o_spec],
        scratch_shapes=[pltpu.VMEM((tile + 2 * _SHORT_HALO, IN_C), F32)],
        compiler_params=_cparams(("parallel",), 32 << 20),
        name="hyena_short_conv",
    )(p_c, p_c, p_c, w, b)


_FILT_W = HYENA_ORDER * HYENA_WIDTH
_EMB_PAD = 128


def _filter_kernel(emb_ref, w1_ref, b1_ref, f1_ref, w2_ref, b2_ref, f2_ref, w3_ref, dl_ref,
                   full_ref, l1_ref, *, tiles_per_dir):
    i = pl.program_id(0)
    emb = emb_ref[...]
    hid = jnp.sin(f1_ref[...] * (jnp.dot(emb, w1_ref[...], preferred_element_type=F32, precision=HIGHEST)
                                 + b1_ref[...]))
    hid = jnp.sin(f2_ref[...] * (jnp.dot(hid, w2_ref[...], preferred_element_type=F32, precision=HIGHEST)
                                 + b2_ref[...]))
    filt = _dot3(*_split(hid), w3_ref[...])
    decay = jnp.exp(-emb[:, 0:1] * dl_ref[...])
    rows = lax.broadcasted_iota(jnp.int32, (filt.shape[0], 1), 0)
    keep = jnp.logical_or(i != tiles_per_dir, rows != 0)
    for o in range(HYENA_ORDER):
        sl = slice(o * HYENA_WIDTH, (o + 1) * HYENA_WIDTH)
        full_ref[:, sl] = jnp.where(keep, filt[:, sl] * decay, 0.0)

    @pl.when(i == 0)
    def _():
        l1_ref[...] = jnp.zeros(l1_ref.shape, F32)

    l1_ref[0:1, :] += jnp.sum(jnp.abs(full_ref[...]), axis=0, keepdims=True)


def hyena_filters(n, w1, b1, f1, w2, b2, f2, w3):
    r = jnp.arange(2 * n, dtype=F32)[:, None]
    pos = jnp.where(r < n, r, 2 * n - r)
    t = pos / (n - 1)
    wpos = 2.0 * math.pi * pos / n
    f = jnp.linspace(1e-4, HYENA_BANDS - 1, HYENA_BANDS, dtype=F32)[None, :]
    emb_full = jnp.concatenate([t, jnp.cos(f * wpos), -jnp.sin(f * wpos),
                                jnp.zeros((2 * n, _EMB_PAD - HYENA_EMB_DIM), F32)], axis=-1)
    w1p = jnp.pad(w1, ((0, _EMB_PAD - HYENA_EMB_DIM), (0, 0)))
    w3d = w3.reshape(HYENA_FILTER_HIDDEN, 2, _FILT_W).transpose(1, 0, 2)
    min_decay = math.log(HYENA_DECAY_TARGET) / HYENA_SLOW_DECAY_PCT
    max_decay = math.log(HYENA_DECAY_TARGET) / HYENA_FAST_DECAY_PCT
    deltas = jnp.abs(jnp.linspace(min_decay, max_decay, HYENA_WIDTH, dtype=F32))[None, :]
    tile = min(512, n)
    tiles_per_dir = n // tile
    hv = pl.BlockSpec((1, HYENA_FILTER_HIDDEN), lambda i: (0, 0))
    return pl.pallas_call(
        functools.partial(_filter_kernel, tiles_per_dir=tiles_per_dir),
        out_shape=[jax.ShapeDtypeStruct((2 * n, _FILT_W), F32),
                   jax.ShapeDtypeStruct((V7X_SUBLANES, _FILT_W), F32)],
        grid=(2 * tiles_per_dir,),
        in_specs=[pl.BlockSpec((tile, _EMB_PAD), lambda i: (i, 0)),
                  pl.BlockSpec((_EMB_PAD, HYENA_FILTER_HIDDEN), lambda i: (0, 0)), hv, hv,
                  pl.BlockSpec((HYENA_FILTER_HIDDEN, HYENA_FILTER_HIDDEN), lambda i: (0, 0)), hv, hv,
                  pl.BlockSpec((None, HYENA_FILTER_HIDDEN, _FILT_W), lambda i: (i // tiles_per_dir, 0, 0)),
                  pl.BlockSpec((1, HYENA_WIDTH), lambda i: (0, 0))],
        out_specs=[pl.BlockSpec((tile, _FILT_W), lambda i: (i, 0)),
                   pl.BlockSpec((V7X_SUBLANES, _FILT_W), lambda i: (0, 0))],
        compiler_params=_cparams(("arbitrary",), 32 << 20),
        name="hyena_filters",
    )(emb_full, w1p, b1, f1, w2, b2, f2, w3d, deltas)


def _fft_tables(n1):
    n2 = FFT_N2
    nb = n1 // 2 + 1
    nbp = -(-nb // V7X_SUBLANES) * V7X_SUBLANES
    k1 = np.arange(nbp)[:, None].astype(np.float64)
    valid = (np.arange(nbp) < nb)[:, None]
    s1 = np.arange(n1)[None, :]
    ang = 2.0 * np.pi * k1 * s1 / n1
    f1 = np.concatenate([np.where(valid, np.cos(ang), 0.0), np.where(valid, -np.sin(ang), 0.0)], axis=0)
    idx = np.arange(n2)
    ang2 = 2.0 * np.pi * np.outer(idx, idx) / n2
    f2 = np.stack([np.cos(ang2), -np.sin(ang2)])
    angt = 2.0 * np.pi * k1 * idx[None, :] / (n1 * n2)
    tw_f = np.stack([np.cos(angt), -np.sin(angt)])[:, :, None, :]
    t1 = np.arange(n1 // 2)[:, None]
    ang1 = 2.0 * np.pi * t1 * np.arange(nbp)[None, :] / n1
    inv1 = np.stack([np.cos(ang1), np.sin(ang1)])
    wk = np.where((np.arange(nbp) == 0) | (np.arange(nbp) == n1 // 2), 1.0, 2.0) * (np.arange(nbp) < nb)
    angb = 2.0 * np.pi * idx[:, None] * np.arange(nbp)[None, :] / (n1 * n2)
    tw_i = np.stack([np.cos(angb) * wk, np.sin(angb) * wk], axis=1) / (n1 * n2)
    as32 = lambda a: jnp.asarray(a, dtype=F32)
    return dict(nb=nb, nbp=nbp, f1=as32(f1), f2=as32(f2), tw_f=as32(tw_f), inv1=as32(inv1), tw_i=as32(tw_i))


def _split(a):
    hi = a.astype(BF16)
    return hi, (a - hi.astype(F32)).astype(BF16)


def _dot3(a_hi, a_lo, b):
    m = a_hi.shape[0]
    b_hi, b_lo = _split(b)
    t = jnp.dot(jnp.concatenate([a_hi, a_lo], axis=0), b_hi, preferred_element_type=F32)
    return t[:m] + t[m:] + jnp.dot(a_hi, b_lo, preferred_element_type=F32)


_FFT_T2 = 8


def _fft1_kernel(x_ref, f_ref, o_ref):
    nbp = o_ref.shape[1]
    a = _dot3(*_split(f_ref[...]), x_ref[...])
    o_ref[0] = a[:nbp]
    o_ref[1] = a[nbp:]


def fft_level1(x, f1, n1):
    rows, ch = x.shape
    s1 = rows // FFT_N2
    cols = FFT_N2 * ch
    nbp = f1.shape[0] // 2
    tc = min(8192, cols)
    out = pl.pallas_call(
        _fft1_kernel,
        out_shape=jax.ShapeDtypeStruct((2, nbp, cols), F32),
        grid=(cols // tc,),
        in_specs=[pl.BlockSpec((s1, tc), lambda j: (0, j)),
                  pl.BlockSpec((2 * nbp, s1), lambda j: (0, 0))],
        out_specs=pl.BlockSpec((2, nbp, tc), lambda j: (0, 0, j)),
        compiler_params=_cparams(("parallel",), 3 * (s1 + 4 * nbp) * tc * 4 + (8 << 20)),
        name="fft_level1",
    )(x.reshape(s1, cols), f1[:, :s1])
    return out.reshape(2, nbp, FFT_N2, ch)


def _fft1_inplace_kernel(x_ref, f_ref, o_ref):
    nbp = o_ref.shape[1]
    f = f_ref[...]
    for j in range(_FFT_T2):
        a = jnp.dot(f, x_ref[:, j, :], preferred_element_type=F32, precision=HIGHEST)
        o_ref[0, :, j, :] = a[:nbp]
        o_ref[1, :, j, :] = a[nbp:]


def fft_level1_inplace(x, f1, n1):
    rows, ch = x.shape
    s1 = rows // FFT_N2
    nbp = f1.shape[0] // 2
    return pl.pallas_call(
        _fft1_inplace_kernel,
        out_shape=jax.ShapeDtypeStruct((2, nbp, FFT_N2, ch), F32),
        grid=(FFT_N2 // _FFT_T2,),
        in_specs=[pl.BlockSpec((s1, _FFT_T2, ch), lambda j: (0, j, 0)),
                  pl.BlockSpec((2 * nbp, s1), lambda j: (0, 0))],
        out_specs=pl.BlockSpec((2, nbp, _FFT_T2, ch), lambda j: (0, 0, j, 0)),
        compiler_params=_cparams(("parallel",), 2 * (s1 + 2 * nbp) * _FFT_T2 * ch * 4 + (16 << 20)),
        name="fft_level1_inplace",
    )(x.reshape(s1, FFT_N2, ch), f1[:, :s1])


def _cdot3(g_hi, g_lo, br, bi):
    m = g_hi.shape[0] // 2
    t1 = _dot3(g_hi, g_lo, br)
    t2 = _dot3(g_hi, g_lo, bi)
    return t1[:m] - t2[m:], t2[:m] + t1[m:]


def _twiddled_dft(f_ref, t_ref):
    fr, fi = f_ref[0], f_ref[1]
    tr, ti = t_ref[0], t_ref[1]
    return _split(jnp.concatenate([fr * tr - fi * ti, fr * ti + fi * tr], axis=0))


def _fft2_spectrum_kernel(a_ref, f_ref, t_ref, l1_ref, o_ref, *, nb):
    k1 = pl.program_id(0)

    @pl.when(k1 < nb)
    def _():
        g_hi, g_lo = _twiddled_dft(f_ref, t_ref)
        xr, xi = _cdot3(g_hi, g_lo, a_ref[0], a_ref[1])
        inv = 1.0 / l1_ref[0:1, :]
        o_ref[0] = xr * inv
        o_ref[1] = xi * inv

    @pl.when(k1 >= nb)
    def _():
        o_ref[...] = jnp.zeros(o_ref.shape, F32)


def fft_level2_spectrum(a, tabs, l1):
    _, nbp, n2, ch = a.shape
    blk = pl.BlockSpec((2, None, n2, ch), lambda k: (0, k, 0, 0))
    return pl.pallas_call(
        functools.partial(_fft2_spectrum_kernel, nb=tabs["nb"]),
        out_shape=jax.ShapeDtypeStruct(a.shape, F32),
        grid=(nbp,),
        in_specs=[blk, pl.BlockSpec((2, n2, n2), lambda k: (0, 0, 0)),
                  pl.BlockSpec((2, None, 1, n2), lambda k: (0, k, 0, 0)),
                  pl.BlockSpec((V7X_SUBLANES, ch), lambda k: (0, 0))],
        out_specs=blk,
        compiler_params=_cparams(("parallel",), 8 * 2 * n2 * ch * 4 + (8 << 20)),
        name="fft_level2_spectrum",
    )(a, tabs["f2"], tabs["tw_f"], l1)


def _fft2_conv_kernel(a_ref, kf_ref, f_ref, t_ref, o_ref, *, nb):
    k1 = pl.program_id(0)

    @pl.when(k1 < nb)
    def _():
        g_hi, g_lo = _twiddled_dft(f_ref, t_ref)
        xr, xi = _cdot3(g_hi, g_lo, a_ref[0], a_ref[1])
        kr, ki = kf_ref[0], kf_ref[1]
        yr = xr * kr - xi * ki
        yi = xr * ki + xi * kr
        n2 = f_ref.shape[1]
        f_hi, f_lo = _split(f_ref[...].reshape(2 * n2, n2))
        u1 = _dot3(f_hi, f_lo, yr)
        u2 = _dot3(f_hi, f_lo, yi)
        o_ref[0] = u1[:n2] + u2[n2:]
        o_ref[1] = u2[:n2] - u1[n2:]

    @pl.when(k1 >= nb)
    def _():
        o_ref[...] = jnp.zeros(o_ref.shape, F32)


def fft_level2_conv(a, kf, order, tabs):
    _, nbp, n2, ch = a.shape
    blk = pl.BlockSpec((2, None, n2, ch), lambda k: (0, k, 0, 0))
    return pl.pallas_call(
        functools.partial(_fft2_conv_kernel, nb=tabs["nb"]),
        out_shape=jax.ShapeDtypeStruct(a.shape, F32),
        grid=(nbp,),
        in_specs=[blk, pl.BlockSpec((2, None, n2, ch), lambda k: (0, k, 0, order)),
                  pl.BlockSpec((2, n2, n2), lambda k: (0, 0, 0)),
                  pl.BlockSpec((2, None, 1, n2), lambda k: (0, k, 0, 0))],
        out_specs=blk,
        compiler_params=_cparams(("parallel",), 10 * 2 * n2 * ch * 4 + (8 << 20)),
        name="fft_level2_conv",
    )(a, kf, tabs["f2"], tabs["tw_f"])


_INV_T2 = 8


def _ifft1_gate_kernel(b_ref, inv_ref, tw_ref, z_ref, g_ref, bias_ref, o_ref):
    ca, sa = inv_ref[0], inv_ref[1]
    for j in range(_INV_T2):
        cb, sb = tw_ref[j, 0:1, :], tw_ref[j, 1:2, :]
        mc = ca * cb - sa * sb
        ms = -(sa * cb + ca * sb)
        y = (jnp.dot(mc, b_ref[0, :, j, :], preferred_element_type=F32, precision=HIGHEST)
             + jnp.dot(ms, b_ref[1, :, j, :], preferred_element_type=F32, precision=HIGHEST))
        z = z_ref[:, j, :]
        o_ref[:, j, :] = (g_ref[:, j, :] * (y + z * bias_ref[...])).astype(o_ref.dtype)


def ifft_level1_gate(b, z, gate, bias, tabs, out_dtype):
    _, nbp, n2, ch = b.shape
    n = z.shape[0]
    s1 = n // n2
    view = lambda a: a.reshape(s1, n2, ch)
    seq = pl.BlockSpec((s1, _INV_T2, ch), lambda j: (0, j, 0))
    out = pl.pallas_call(
        _ifft1_gate_kernel,
        out_shape=jax.ShapeDtypeStruct((s1, n2, ch), out_dtype),
        grid=(n2 // _INV_T2,),
        in_specs=[pl.BlockSpec((2, nbp, _INV_T2, ch), lambda j: (0, 0, j, 0)),
                  pl.BlockSpec((2, s1, nbp), lambda j: (0, 0, 0)),
                  pl.BlockSpec((_INV_T2, 2, nbp), lambda j: (j, 0, 0)),
                  seq, seq, pl.BlockSpec((1, ch), lambda j: (0, 0))],
        out_specs=seq,
        compiler_params=_cparams(("parallel",), 2 * (2 * nbp + 3 * s1) * _INV_T2 * ch * 4 + (8 << 20)),
        name="ifft_level1_gate",
    )(b, tabs["inv1"], tabs["tw_i"], view(z), view(gate), bias)
    return out.reshape(n, ch)


def hyena_long_conv_fft(x1, x2, v, full, l1, bias):
    n, ch = v.shape
    n1 = 2 * n // FFT_N2
    tabs = _fft_tables(n1)
    kf = fft_level2_spectrum(fft_level1_inplace(full, tabs["f1"], n1), tabs, l1)
    z = v
    for o, gate in enumerate((x1, x2)):
        a = fft_level1(z, tabs["f1"], n1)
        b = fft_level2_conv(a, kf, o, tabs)
        last = o == HYENA_ORDER - 1
        z = ifft_level1_gate(b, z, gate, bias[o:o + 1], tabs, BF16 if last else F32)
    return z


def _direct_conv_kernel(lo_ref, hi_ref, l1_ref, z_ref, g_ref, bias_ref, o_ref, ebuf, *, n):
    inv = 1.0 / l1_ref[0:1, :]
    ebuf[0:n, :] = hi_ref[...] * inv
    ebuf[n:2 * n, :] = lo_ref[...] * inv
    for c0 in range(0, HYENA_WIDTH, V7X_LANES):
        cs = slice(c0, c0 + V7X_LANES)

        def body(g, acc, cs=cs):
            s0 = pl.multiple_of(g * V7X_SUBLANES, V7X_SUBLANES)
            base = pl.multiple_of(n - V7X_SUBLANES - s0, V7X_SUBLANES)
            win = ebuf[pl.ds(base, n + V7X_SUBLANES), cs]
            zb = z_ref[pl.ds(s0, V7X_SUBLANES), cs]
            for j in range(V7X_SUBLANES):
                acc = acc + win[V7X_SUBLANES - j:V7X_SUBLANES - j + n, :] * zb[j:j + 1, :]
            return acc

        y = lax.fori_loop(0, n // V7X_SUBLANES, body, jnp.zeros((n, V7X_LANES), F32))
        o_ref[:, cs] = (g_ref[:, cs] * (y + z_ref[:, cs] * bias_ref[:, cs])).astype(o_ref.dtype)


def hyena_long_conv_direct(x1, x2, v, full, l1, bias):
    n, ch = v.shape
    z = v
    for o, gate in enumerate((x1, x2)):
        last = o == HYENA_ORDER - 1
        seq = pl.BlockSpec((n, ch), lambda i: (0, 0))
        z = pl.pallas_call(
            functools.partial(_direct_conv_kernel, n=n),
            out_shape=jax.ShapeDtypeStruct((n, ch), BF16 if last else F32),
            grid=(1,),
            in_specs=[pl.BlockSpec((n, ch), lambda i, o=o: (0, o)),
                      pl.BlockSpec((n, ch), lambda i, o=o: (1, o)),
                      pl.BlockSpec((V7X_SUBLANES, ch), lambda i, o=o: (0, o)),
                      seq, seq, pl.BlockSpec((1, ch), lambda i: (0, 0))],
            out_specs=seq,
            scratch_shapes=[pltpu.VMEM((2 * n, ch), F32)],
            compiler_params=_cparams(("arbitrary",), 32 << 20),
            name="hyena_direct_conv",
        )(full, full, l1, z, gate, bias[o:o + 1])
    return z


_DIRECT_CONV_MAX = 512


def hyena_mix(p_c, short_w, short_b, w1, b1, f1, w2, b2, f2, w3, bias):
    n = p_c.shape[0]
    x1, x2, v = hyena_short_conv(p_c, short_w, short_b)
    full, l1 = hyena_filters(n, w1, b1, f1, w2, b2, f2, w3)
    if n <= _DIRECT_CONV_MAX:
        return hyena_long_conv_direct(x1, x2, v, full, l1, bias)
    return hyena_long_conv_fft(x1, x2, v, full, l1, bias)


def _out_kernel(cv_ref, at_ref, hy_ref, w_ref, x_ref, g_ref, gate_ref, o_ref):
    y = jnp.dot(cv_ref[...], w_ref[0:CONV_WIDTH, :], preferred_element_type=F32)
    y = y + jnp.dot(at_ref[...], w_ref[CONV_WIDTH:CONV_WIDTH + MLA_WIDTH, :], preferred_element_type=F32)
    y = y + jnp.dot(hy_ref[...], w_ref[CONV_WIDTH + MLA_WIDTH:, :], preferred_element_type=F32)
    o_ref[...] = x_ref[...] + gate_ref[...] * _rms(y, g_ref[...])


def out_projection(conv_o, attn, hy_o, w_out_bf16, x, g, gate):
    n, d = x.shape
    tm = min(512, n)
    mix = w_out_bf16.shape[0]
    vec = pl.BlockSpec((1, d), lambda i: (0, 0))
    row = lambda w: pl.BlockSpec((tm, w), lambda i: (i, 0))
    return pl.pallas_call(
        _out_kernel,
        out_shape=jax.ShapeDtypeStruct((n, d), F32),
        grid=(n // tm,),
        in_specs=[row(CONV_WIDTH), row(MLA_WIDTH), row(HYENA_WIDTH),
                  _resident((mix, d), lambda i: (0, 0)), row(d), vec, vec],
        out_specs=row(d),
        compiler_params=_cparams(("parallel",), mix * d * 2 + 6 * tm * d * 4 + (8 << 20)),
        name="out_projection",
    )(conv_o, attn, hy_o, w_out_bf16, x, g, gate)


def _ffn_kernel(x_ref, gpre_ref, sh_ref, sc_ref, w1_ref, w2_ref, gpost_ref, gate_ref, o_ref, h_sc):
    f = pl.program_id(1)

    @pl.when(f == 0)
    def _():
        h = _rms(x_ref[...], gpre_ref[...]) * (1.0 + sc_ref[...]) + sh_ref[...]
        h_sc[...] = h.astype(BF16)
        o_ref[...] = jnp.zeros(o_ref.shape, F32)

    a = jnp.dot(h_sc[...], w1_ref[...].astype(BF16), preferred_element_type=F32)
    a = jnp.square(jnp.maximum(a, 0.0)).astype(BF16)
    o_ref[...] += jnp.dot(a, w2_ref[...].astype(BF16), preferred_element_type=F32)

    @pl.when(f == pl.num_programs(1) - 1)
    def _():
        o_ref[...] = x_ref[...] + gate_ref[...] * _rms(o_ref[...], gpost_ref[...])


def ffn(x, g_pre, shift, scale, w_ff1, w_ff2, layer, g_post, gate):
    n, d = x.shape
    dff = w_ff1.shape[2]
    tm = min(1024, n)
    tf = 512
    vec = pl.BlockSpec((1, d), lambda i, f: (0, 0))
    vmem = 3 * tm * d * 4 + tm * d * 2 + 4 * d * tf * 4 + 2 * d * tf * 2 + 3 * tm * tf * 4 + (4 << 20)
    return pl.pallas_call(
        _ffn_kernel,
        out_shape=jax.ShapeDtypeStruct((n, d), F32),
        grid=(n // tm, dff // tf),
        in_specs=[pl.BlockSpec((tm, d), lambda i, f: (i, 0), pipeline_mode=pl.Buffered(1)), vec, vec, vec,
                  pl.BlockSpec((None, d, tf), lambda i, f: (layer, 0, f)),
                  pl.BlockSpec((None, tf, d), lambda i, f: (layer, f, 0)), vec, vec],
        out_specs=pl.BlockSpec((tm, d), lambda i, f: (i, 0)),
        scratch_shapes=[pltpu.VMEM((tm, d), BF16)],
        compiler_params=_cparams(("parallel", "arbitrary"), vmem),
        name="ffn",
    )(x, g_pre, shift, scale, w_ff1, w_ff2, g_post, gate)


def _rot_cols(w):
    half = w.shape[-1] // 2
    return jnp.concatenate([-w[..., half:], w[..., :half]], axis=-1)


def _rope_table(rows):
    row = jnp.repeat(jnp.arange(rows, dtype=F32), GRID_W)
    col = jnp.tile(jnp.arange(GRID_W, dtype=F32), rows)
    axis_dim = QK_ROPE_DIM // 2
    inv = 1.0 / (ROPE_BASE ** (jnp.arange(0, axis_dim, 2, dtype=F32) / axis_dim))
    ang = jnp.concatenate([row[:, None] * inv, col[:, None] * inv], axis=-1)
    cos, sin = jnp.cos(ang), jnp.sin(ang)
    return jnp.concatenate([cos, cos, sin, sin], axis=-1)


def kernel(x, c, ctx, c_ctx, w_mod, b_mod, g_pre_mix, g_post_mix, g_pre_ffn, g_post_ffn, w_in, conv_dw_w, conv_dw_b, conv_ln_g, conv_ln_b, mla_q_norm, mla_w_uq, mla_kv_norm, mla_w_ukv, hy_short_w, hy_short_b, hy_w1, hy_b1, hy_freq1, hy_w2, hy_b2, hy_freq2, hy_w3, hy_bias, w_out, w_ff1, w_ff2):
    batch, seq, d = x.shape
    assert batch == 1 and c.shape[0] == 1 and ctx.shape[0] == 1
    depth = w_mod.shape[0]
    ctx_len = ctx.shape[1]
    xs, xc = x[0], ctx[0]

    mod = modulation(jnp.stack([c[0], c_ctx]), w_mod, b_mod)

    cs_lat = _rope_table(seq // GRID_W)
    cs_ctx = jnp.concatenate([jnp.ones((ctx_len, QK_ROPE_DIM), F32), jnp.zeros((ctx_len, QK_ROPE_DIM), F32)], -1)
    row = lambda a: a.reshape(1, -1)
    qa, qb_end = Q_LORA_RANK, Q_LORA_RANK + KV_LORA_RANK

    for l in range(depth):
        last = l == depth - 1
        mvec = lambda r, j: mod[l, r:r + 1, j * d:(j + 1) * d]
        wl = w_in[l]
        w_b = wl[:, IN_A:IN_A + IN_B]
        w_kr = w_b[:, qb_end:]
        w_in_p = jnp.concatenate([wl[:, :IN_A], wl[:, IN_A + IN_B:], w_b[:, :qb_end], w_kr, _rot_cols(w_kr)],
                                 axis=-1).astype(BF16)
        wq = mla_w_uq[l].reshape(Q_LORA_RANK, MLA_HEADS, QK_DIM).transpose(1, 0, 2)
        wq = jnp.concatenate([wq, _rot_cols(wq[..., QK_NOPE_DIM:])], axis=-1)
        wqt = wq.transpose(0, 2, 1).astype(BF16)
        wkv = mla_w_ukv[l].reshape(KV_LORA_RANK, MLA_HEADS, QK_NOPE_DIM + V_HEAD_DIM).transpose(1, 0, 2)
        wk = wkv[..., :QK_NOPE_DIM].astype(BF16)
        wvt = wkv[..., QK_NOPE_DIM:].transpose(0, 2, 1).astype(BF16)
        w_out_b = w_out[l].astype(BF16)

        def project(stream, r, cs):
            p_a, p_c, c_q, c_kv, kr = in_projection(stream, row(g_pre_mix[l]), mvec(r, 0), mvec(r, 1), w_in_p)
            qt, k, vt = mla_qkv(c_q, c_kv, kr, cs, cs.T, row(mla_q_norm[l]), row(mla_kv_norm[l]), wqt, wk, wvt)
            return p_a, p_c, qt, k, vt

        def finish(stream, r, p_a, p_c, attn):
            conv_o = conformer_conv(p_a, conv_dw_w[l], row(conv_dw_b[l]), row(conv_ln_g[l]), row(conv_ln_b[l]))
            hy_o = hyena_mix(p_c, hy_short_w[l], row(hy_short_b[l]), hy_w1[l], row(hy_b1[l]), row(hy_freq1[l]),
                             hy_w2[l], row(hy_b2[l]), row(hy_freq2[l]), hy_w3[l], hy_bias[l])
            y = out_projection(conv_o, attn, hy_o, w_out_b, stream, row(g_post_mix[l]), mvec(r, 2))
            return ffn(y, row(g_pre_ffn[l]), mvec(r, 3), mvec(r, 4), w_ff1, w_ff2, l, row(g_post_ffn[l]), mvec(r, 5))

        pc_a, pc_c, qc, kc, vc = project(xc, 1, cs_ctx)
        p_a, p_c, q, k, v = project(xs, 0, cs_lat)
        attn = attention(q, [(kc, vc), (k, v)])
        new_x = finish(xs, 0, p_a, p_c, attn)
        if not last:
            xc = finish(xc, 1, pc_a, pc_c, attention(qc, [(kc, vc)]))
        xs = new_x
    return xs[None]
```

```python
import functools
import math

import numpy as np
import jax
import jax.numpy as jnp
from jax import lax
from jax.experimental import pallas as pl
from jax.experimental.pallas import tpu as pltpu

GRID_W = 64
CONV_WIDTH = 512
CONV_TAPS = 31
MLA_HEADS = 8
QK_NOPE_DIM = 128
QK_ROPE_DIM = 64
V_HEAD_DIM = 128
Q_LORA_RANK = 512
KV_LORA_RANK = 256
MLA_WIDTH = MLA_HEADS * V_HEAD_DIM
HYENA_WIDTH = 512
HYENA_ORDER = 2
HYENA_SHORT_TAPS = 3
HYENA_EMB_DIM = 33
HYENA_BANDS = (HYENA_EMB_DIM - 1) // 2
HYENA_FILTER_HIDDEN = 64
HYENA_FAST_DECAY_PCT = 0.3
HYENA_SLOW_DECAY_PCT = 1.5
HYENA_DECAY_TARGET = 1e-2
N_MOD = 6
ROPE_BASE = 10000.0
NORM_EPS = 1e-6
IN_A = 2 * CONV_WIDTH
IN_B = Q_LORA_RANK + KV_LORA_RANK + QK_ROPE_DIM
IN_C = (HYENA_ORDER + 1) * HYENA_WIDTH
QK_DIM = QK_NOPE_DIM + QK_ROPE_DIM

V7X_LANES = 128
V7X_SUBLANES = 8
V7X_VMEM_BYTES = 64 * 1024 * 1024
V7X_VMEM_BUDGET = 56 * 1024 * 1024

FFT_N2 = 256

F32 = jnp.float32
BF16 = jnp.bfloat16
HIGHEST = lax.Precision.HIGHEST


def _cparams(semantics, vmem_bytes):
    return pltpu.CompilerParams(dimension_semantics=semantics,
                                vmem_limit_bytes=int(min(max(vmem_bytes, 16 << 20), V7X_VMEM_BUDGET)))


def _resident(block_shape, index_map):
    return pl.BlockSpec(block_shape, index_map, pipeline_mode=pl.Buffered(1))


def _rms(x, g):
    ms = jnp.mean(x * x, axis=-1, keepdims=True)
    return x * lax.rsqrt(ms + NORM_EPS) * g


def _silu(x):
    return x * jax.nn.sigmoid(x)


_MOD_VECS = 2


def _mod_kernel(cb_ref, w_ref, b_ref, o_ref, s_sc):
    d, tn = w_ref.shape
    reps = tn // V7X_LANES

    @pl.when(jnp.logical_and(pl.program_id(0) == 0, pl.program_id(1) == 0))
    def _():
        s_sc[...] = _silu(cb_ref[...])

    def body(kb, accs):
        k0 = pl.multiple_of(kb * V7X_SUBLANES, V7X_SUBLANES)
        wblk = w_ref[pl.ds(k0, V7X_SUBLANES), :]
        out = []
        for r in range(_MOD_VECS):
            s = s_sc[r, pl.ds(k0, V7X_SUBLANES), :]
            out.append(accs[r] + wblk * jnp.concatenate([s] * reps, axis=1))
        return tuple(out)

    zero = jnp.zeros((V7X_SUBLANES, tn), F32)
    accs = lax.fori_loop(0, d // V7X_SUBLANES, body, (zero,) * _MOD_VECS)
    o_ref[...] = jnp.zeros(o_ref.shape, F32)
    for r in range(_MOD_VECS):
        o_ref[r:r + 1, :] = jnp.sum(accs[r], axis=0, keepdims=True) + b_ref[...]


def modulation(cvecs, w_mod, b_mod):
    n_layers, d, width = w_mod.shape
    tn = 2048
    cb = jnp.broadcast_to(cvecs[:, :, None], (_MOD_VECS, d, V7X_LANES))
    return pl.pallas_call(
        _mod_kernel,
        out_shape=jax.ShapeDtypeStruct((n_layers, V7X_SUBLANES, width), F32),
        grid=(n_layers, width // tn),
        in_specs=[pl.BlockSpec((_MOD_VECS, d, V7X_LANES), lambda l, j: (0, 0, 0)),
                  pl.BlockSpec((None, d, tn), lambda l, j: (l, 0, j)),
                  pl.BlockSpec((None, 1, tn), lambda l, j: (l, 0, j))],
        out_specs=pl.BlockSpec((None, V7X_SUBLANES, tn), lambda l, j: (l, 0, j)),
        scratch_shapes=[pltpu.VMEM((_MOD_VECS, d, V7X_LANES), F32)],
        compiler_params=_cparams(("arbitrary", "arbitrary"), 2 * d * tn * 4 + (8 << 20)),
        name="modulation",
    )(cb, w_mod, b_mod.reshape(n_layers, 1, width))


_IN_SPLITS = (IN_A, IN_C, Q_LORA_RANK, KV_LORA_RANK, 2 * QK_ROPE_DIM)
_IN_WIDTH = sum(_IN_SPLITS)


def _in_kernel(x_ref, g_ref, sh_ref, sc_ref, w_ref, *o_refs):
    h = _rms(x_ref[...], g_ref[...]) * (1.0 + sc_ref[...]) + sh_ref[...]
    hb = h.astype(BF16)
    off = 0
    for o_ref, width in zip(o_refs, _IN_SPLITS):
        o_ref[...] = jnp.dot(hb, w_ref[:, off:off + width], preferred_element_type=F32)
        off += width


def in_projection(x, g, shift, scale, w_in_bf16):
    n, d = x.shape
    tm = min(512, n)
    vec = pl.BlockSpec((1, d), lambda i: (0, 0))
    vmem = 2 * tm * d * 4 + d * _IN_WIDTH * 2 + 3 * tm * _IN_WIDTH * 4 + (4 << 20)
    return pl.pallas_call(
        _in_kernel,
        out_shape=[jax.ShapeDtypeStruct((n, w), F32) for w in _IN_SPLITS],
        grid=(n // tm,),
        in_specs=[pl.BlockSpec((tm, d), lambda i: (i, 0)), vec, vec, vec,
                  _resident((d, _IN_WIDTH), lambda i: (0, 0))],
        out_specs=[pl.BlockSpec((tm, w), lambda i: (i, 0)) for w in _IN_SPLITS],
        compiler_params=_cparams(("parallel",), vmem),
        name="in_projection",
    )(x, g, shift, scale, w_in_bf16)


_CONV_HALO = 16
_CONV_ROWS = 32


def _conv_kernel(prev_ref, cur_ref, next_ref, w_ref, b_ref, lg_ref, lb_ref, o_ref, ybuf, cbuf, *, tile):
    i = pl.program_id(0)
    last = pl.num_programs(0) - 1

    def glu(v):
        return v[:, :CONV_WIDTH] * jax.nn.sigmoid(v[:, CONV_WIDTH:])

    ybuf[0, 0:_CONV_HALO, :] = jnp.where(i > 0, glu(prev_ref[...]), 0.0)
    ybuf[0, _CONV_HALO:_CONV_HALO + tile, :] = glu(cur_ref[...])
    ybuf[0, _CONV_HALO + tile:2 * _CONV_HALO + tile, :] = jnp.where(i < last, glu(next_ref[...]), 0.0)
    span = tile + 2 * _CONV_HALO - V7X_SUBLANES
    for c in range(1, V7X_SUBLANES):
        ybuf[c, 0:span, :] = ybuf[0, c:c + span, :]

    pad = (CONV_TAPS - 1) // 2
    groups = _CONV_ROWS // V7X_SUBLANES

    def chunk(r, carry):
        r0 = pl.multiple_of(r * _CONV_ROWS, _CONV_ROWS)
        acc = [jnp.zeros((V7X_SUBLANES, CONV_WIDTH), F32)] * groups
        for k in range(CONV_TAPS):
            a, c = divmod(_CONV_HALO - pad + k, V7X_SUBLANES)
            wk = w_ref[k]
            for g in range(groups):
                rows = pl.ds(pl.multiple_of(r0 + (a + g) * V7X_SUBLANES, V7X_SUBLANES), V7X_SUBLANES)
                acc[g] = acc[g] + ybuf[c, rows, :] * wk
        for g in range(groups):
            cbuf[pl.ds(pl.multiple_of(r0 + g * V7X_SUBLANES, V7X_SUBLANES), V7X_SUBLANES), :] = acc[g]
        return carry

    lax.fori_loop(0, tile // _CONV_ROWS, chunk, 0)

    y = cbuf[...] + b_ref[...]
    mu = jnp.mean(y, axis=-1, keepdims=True)
    yc = y - mu
    var = jnp.mean(yc * yc, axis=-1, keepdims=True)
    z = yc * lax.rsqrt(var + NORM_EPS) * lg_ref[...] + lb_ref[...]
    o_ref[...] = _silu(z).astype(o_ref.dtype)


def conformer_conv(p_a, dw_w, dw_b, ln_g, ln_b):
    n = p_a.shape[0]
    tile = min(512, n)
    hb = tile // _CONV_HALO
    nhb = n // _CONV_HALO
    vec = pl.BlockSpec((1, CONV_WIDTH), lambda i: (0, 0))
    return pl.pallas_call(
        functools.partial(_conv_kernel, tile=tile),
        out_shape=jax.ShapeDtypeStruct((n, CONV_WIDTH), BF16),
        grid=(n // tile,),
        in_specs=[pl.BlockSpec((_CONV_HALO, IN_A), lambda i: (jnp.maximum(i * hb - 1, 0), 0)),
                  pl.BlockSpec((tile, IN_A), lambda i: (i, 0)),
                  pl.BlockSpec((_CONV_HALO, IN_A), lambda i: (jnp.minimum((i + 1) * hb, nhb - 1), 0)),
                  pl.BlockSpec((CONV_TAPS, V7X_SUBLANES, CONV_WIDTH), lambda i: (0, 0, 0)), vec, vec, vec],
        out_specs=pl.BlockSpec((tile, CONV_WIDTH), lambda i: (i, 0)),
        scratch_shapes=[pltpu.VMEM((V7X_SUBLANES, tile + 2 * _CONV_HALO, CONV_WIDTH), F32),
                        pltpu.VMEM((tile, CONV_WIDTH), F32)],
        compiler_params=_cparams(("parallel",), 4 * tile * IN_A * 4 + 9 * tile * CONV_WIDTH * 4 + (8 << 20)),
        name="conformer_conv",
    )(p_a, p_a, p_a, jnp.repeat(dw_w[:, None, :], V7X_SUBLANES, axis=1), dw_b, ln_g, ln_b)


def _rope_halves(t):
    return t[:, :QK_ROPE_DIM] + t[:, QK_ROPE_DIM:]


_NT = (((1,), (1,)), ((), ()))
_ATTN_QSCALE = (QK_DIM ** -0.5) * math.log2(math.e)
_VT_ROWS = V_HEAD_DIM + 16


def _qkv_kernel(cq_ref, ckv_ref, kr_ref, cs_ref, cst_ref, gq_ref, gkv_ref, wqt_ref, wk_ref, wvt_ref,
                qt_ref, k_ref, vt_ref):
    cqn = _rms(cq_ref[...], gq_ref[...]).astype(BF16)
    ckvn = _rms(ckv_ref[...], gkv_ref[...]).astype(BF16)
    cst = cst_ref[...]
    k_rope = _rope_halves(kr_ref[...] * cs_ref[...]).astype(k_ref.dtype)
    ones = jnp.ones((_VT_ROWS - V_HEAD_DIM, cqn.shape[0]), vt_ref.dtype)
    for h in range(MLA_HEADS):
        qa = lax.dot_general(wqt_ref[h], cqn, _NT, preferred_element_type=F32)
        qa = qa * _ATTN_QSCALE
        qt_ref[h, 0:QK_NOPE_DIM, :] = qa[:QK_NOPE_DIM].astype(qt_ref.dtype)
        t = qa[QK_NOPE_DIM:] * cst
        qt_ref[h, QK_NOPE_DIM:QK_DIM, :] = (t[:QK_ROPE_DIM] + t[QK_ROPE_DIM:]).astype(qt_ref.dtype)
        k_ref[h, :, 0:QK_NOPE_DIM] = jnp.dot(ckvn, wk_ref[h], preferred_element_type=F32).astype(k_ref.dtype)
        k_ref[h, :, QK_NOPE_DIM:QK_DIM] = k_rope
        vt_ref[h, 0:V_HEAD_DIM, :] = lax.dot_general(wvt_ref[h], ckvn, _NT,
                                                     preferred_element_type=F32).astype(vt_ref.dtype)
        vt_ref[h, V_HEAD_DIM:, :] = ones


def mla_qkv(c_q, c_kv, kr, cs, cst, g_q, g_kv, wqt, wk, wvt):
    n = c_q.shape[0]
    tm = min(512, n)
    row = lambda w: pl.BlockSpec((tm, w), lambda i: (i, 0))
    full = lambda a: pl.BlockSpec(a.shape, lambda i: (0,) * a.ndim)
    return pl.pallas_call(
        _qkv_kernel,
        out_shape=[jax.ShapeDtypeStruct((MLA_HEADS, QK_DIM, n), BF16),
                   jax.ShapeDtypeStruct((MLA_HEADS, n, QK_DIM), BF16),
                   jax.ShapeDtypeStruct((MLA_HEADS, _VT_ROWS, n), BF16)],
        grid=(n // tm,),
        in_specs=[row(Q_LORA_RANK), row(KV_LORA_RANK), row(2 * QK_ROPE_DIM), row(2 * QK_ROPE_DIM),
                  pl.BlockSpec((2 * QK_ROPE_DIM, tm), lambda i: (0, i)),
                  full(g_q), full(g_kv), full(wqt), full(wk), full(wvt)],
        out_specs=[pl.BlockSpec((MLA_HEADS, QK_DIM, tm), lambda i: (0, 0, i)),
                   pl.BlockSpec((MLA_HEADS, tm, QK_DIM), lambda i: (0, i, 0)),
                   pl.BlockSpec((MLA_HEADS, _VT_ROWS, tm), lambda i: (0, 0, i))],
        compiler_params=_cparams(("parallel",), 32 << 20),
        name="mla_qkv",
    )(c_q, c_kv, kr, cs, cst, g_q, g_kv, wqt, wk, wvt)


def _attn_kernel(*refs, chunks):
    n_src = len(chunks)
    qt_ref = refs[0]
    kv_refs = refs[1:1 + 2 * n_src]
    o_ref = refs[1 + 2 * n_src]
    m_sc, acc_sc, st0, st1, mx0, mx1 = refs[2 + 2 * n_src:]
    st_sc, mx_sc = (st0, st1), (mx0, mx1)
    qt = qt_ref[...]
    m_sc[...] = jnp.full(m_sc.shape, -jnp.inf, F32)
    acc_sc[...] = jnp.zeros(acc_sc.shape, F32)

    def scores(kb, slot):
        tk = kb.shape[0]
        st = jnp.dot(kb, qt, preferred_element_type=F32)
        st_sc[slot][0:tk, :] = st
        mx_sc[slot][...] = jnp.max(st, axis=0, keepdims=True)

    def softmax_pv(slot, vtb):
        tk = vtb.shape[1]
        m_prev = m_sc[...]
        m_new = jnp.maximum(m_prev, mx_sc[slot][...])
        alpha = jnp.exp2(m_prev - m_new)
        p = jnp.exp2((st_sc[slot][0:tk, :] - m_new).astype(BF16))
        acc_sc[...] = alpha * acc_sc[...] + jnp.dot(vtb, p, preferred_element_type=F32)
        m_sc[...] = m_new

    pending = None
    for s_idx, (length, tk) in enumerate(chunks):
        k_ref, vt_ref = kv_refs[2 * s_idx], kv_refs[2 * s_idx + 1]
        if length == tk:
            if pending is not None:
                softmax_pv(1, pending[...])
            scores(k_ref[...], 1)
            pending = vt_ref
            continue
        steps = length // tk
        assert steps % 2 == 0

        def kblk(j, k_ref=k_ref, tk=tk):
            return k_ref[pl.ds(pl.multiple_of(j * tk, tk), tk), :]

        def vblk(j, vt_ref=vt_ref, tk=tk):
            return vt_ref[:, pl.ds(pl.multiple_of(j * tk, tk), tk)]

        scores(kblk(0), 0)
        if pending is not None:
            softmax_pv(1, pending[...])
            pending = None

        def pair(jj, carry):
            j0 = 2 * jj
            scores(kblk(j0 + 1), 1)
            softmax_pv(0, vblk(j0))
            scores(kblk(j0 + 2), 0)
            softmax_pv(1, vblk(j0 + 1))
            return carry

        lax.fori_loop(0, steps // 2 - 1, pair, 0)
        scores(kblk(steps - 1), 1)
        softmax_pv(0, vblk(steps - 2))
        softmax_pv(1, vblk(steps - 1))
    if pending is not None:
        softmax_pv(1, pending[...])

    acc = acc_sc[...]
    o = acc[:V_HEAD_DIM] / acc[V_HEAD_DIM:V_HEAD_DIM + 1]
    o_ref[...] = o.T.astype(o_ref.dtype)


_ATTN_TQ = 512
_ATTN_TK = 1024


def attention(qt, kv_sources):
    heads, _, lq = qt.shape
    tq = min(_ATTN_TQ, lq)
    chunks = []
    in_specs = [pl.BlockSpec((None, QK_DIM, tq), lambda h, i: (h, 0, i))]
    operands = [qt]
    kv_bytes = 0
    for k, vt in kv_sources:
        lk = k.shape[1]
        chunks.append((lk, min(_ATTN_TK, lk)))
        in_specs.append(pl.BlockSpec((None, lk, QK_DIM), lambda h, i: (h, 0, 0)))
        in_specs.append(pl.BlockSpec((None, _VT_ROWS, lk), lambda h, i: (h, 0, 0)))
        operands += [k, vt]
        kv_bytes += lk * (2 * V7X_LANES + _VT_ROWS) * 2
    return pl.pallas_call(
        functools.partial(_attn_kernel, chunks=tuple(chunks)),
        out_shape=jax.ShapeDtypeStruct((lq, heads * V_HEAD_DIM), BF16),
        grid=(heads, lq // tq),
        in_specs=in_specs,
        out_specs=pl.BlockSpec((tq, V_HEAD_DIM), lambda h, i: (i, h)),
        scratch_shapes=[pltpu.VMEM((1, tq), F32), pltpu.VMEM((_VT_ROWS, tq), F32),
                        pltpu.VMEM((_ATTN_TK, tq), F32), pltpu.VMEM((_ATTN_TK, tq), F32),
                        pltpu.VMEM((1, tq), F32), pltpu.VMEM((1, tq), F32)],
        compiler_params=_cparams(("parallel", "arbitrary"), 2 * kv_bytes + (16 << 20)),
        name="attention",
    )(*operands)


_SHORT_HALO = 8


def _short_kernel(prev_ref, cur_ref, next_ref, w_ref, b_ref, x1_ref, x2_ref, v_ref, ubuf, *, tile):
    i = pl.program_id(0)
    last = pl.num_programs(0) - 1
    ubuf[0:_SHORT_HALO, :] = jnp.where(i > 0, prev_ref[...], 0.0)
    ubuf[_SHORT_HALO:_SHORT_HALO + tile, :] = cur_ref[...]
    ubuf[_SHORT_HALO + tile:2 * _SHORT_HALO + tile, :] = jnp.where(i < last, next_ref[...], 0.0)
    pad = (HYENA_SHORT_TAPS - 1) // 2
    w = w_ref[...]
    for o_ref, c0 in ((x1_ref, 0), (x2_ref, HYENA_WIDTH), (v_ref, 2 * HYENA_WIDTH)):
        acc = jnp.zeros((tile, HYENA_WIDTH), F32) + b_ref[:, c0:c0 + HYENA_WIDTH]
        for k in range(HYENA_SHORT_TAPS):
            acc = acc + (ubuf[_SHORT_HALO - pad + k:_SHORT_HALO - pad + k + tile, c0:c0 + HYENA_WIDTH]
                         * w[k:k + 1, c0:c0 + HYENA_WIDTH])
        o_ref[...] = acc


def hyena_short_conv(p_c, w, b):
    n = p_c.shape[0]
    tile = min(256, n)
    hb = tile // _SHORT_HALO
    nhb = n // _SHORT_HALO
    out = jax.ShapeDtypeStruct((n, HYENA_WIDTH), F32)
    o_spec = pl.BlockSpec((tile, HYENA_WIDTH), lambda i: (i, 0))
    return pl.pallas_call(
        functools.partial(_short_kernel, tile=tile),
        out_shape=[out, out, out],
        grid=(n // tile,),
        in_specs=[pl.BlockSpec((_SHORT_HALO, IN_C), lambda i: (jnp.maximum(i * hb - 1, 0), 0)),
                  pl.BlockSpec((tile, IN_C), lambda i: (i, 0)),
                  pl.BlockSpec((_SHORT_HALO, IN_C), lambda i: (jnp.minimum((i + 1) * hb, nhb - 1), 0)),
                  pl.BlockSpec((HYENA_SHORT_TAPS, IN_C), lambda i: (0, 0)),
                  pl.BlockSpec((1, IN_C), lambda i: (0, 0))],
        out_specs=[o_spec, o_spec, o_spec],
        scratch_shapes=[pltpu.VMEM((tile + 2 * _SHORT_HALO, IN_C), F32)],
        compiler_params=_cparams(("parallel",), 32 << 20),
        name="hyena_short_conv",
    )(p_c, p_c, p_c, w, b)


_FILT_W = HYENA_ORDER * HYENA_WIDTH
_EMB_PAD = 128


_TN = (((0,), (0,)), ((), ()))


def _filter_kernel(embt_ref, t_ref, w1t_ref, b1_ref, f1_ref, w2t_ref, b2_ref, f2_ref, w3_ref, dl_ref,
                   full_ref, l1_ref, *, tiles_per_dir):
    i = pl.program_id(0)
    tile = embt_ref.shape[1]
    hid = jnp.sin(f1_ref[...] * (jnp.dot(w1t_ref[...], embt_ref[...], preferred_element_type=F32,
                                         precision=HIGHEST) + b1_ref[...]))
    hid = jnp.sin(f2_ref[...] * (jnp.dot(w2t_ref[...], hid, preferred_element_type=F32,
                                         precision=HIGHEST) + b2_ref[...]))
    h_hi, h_lo = _split(hid)
    w_hi, w_lo = _split(w3_ref[...])
    t2 = lax.dot_general(jnp.concatenate([h_hi, h_lo], axis=1), w_hi, _TN, preferred_element_type=F32)
    filt = t2[:tile] + t2[tile:] + lax.dot_general(h_hi, w_lo, _TN, preferred_element_type=F32)
    decay = jnp.exp(-t_ref[...] * dl_ref[...])
    rows = lax.broadcasted_iota(jnp.int32, (filt.shape[0], 1), 0)
    keep = jnp.logical_or(i != tiles_per_dir, rows != 0)
    for o in range(HYENA_ORDER):
        sl = slice(o * HYENA_WIDTH, (o + 1) * HYENA_WIDTH)
        full_ref[:, sl] = jnp.where(keep, filt[:, sl] * decay, 0.0)

    @pl.when(i == 0)
    def _():
        l1_ref[...] = jnp.zeros(l1_ref.shape, F32)

    l1_ref[0:1, :] += jnp.sum(jnp.abs(full_ref[...]), axis=0, keepdims=True)


def hyena_filters(n, w1, b1, f1, w2, b2, f2, w3):
    r = jnp.arange(2 * n, dtype=F32)
    pos = jnp.where(r < n, r, 2 * n - r)
    t = pos / (n - 1)
    wpos = (2.0 * math.pi * pos / n)[None, :]
    f = jnp.linspace(1e-4, HYENA_BANDS - 1, HYENA_BANDS, dtype=F32)[:, None]
    emb_t = jnp.concatenate([t[None, :], jnp.cos(f * wpos), -jnp.sin(f * wpos),
                             jnp.zeros((_EMB_PAD - HYENA_EMB_DIM, 2 * n), F32)], axis=0)
    w1t = jnp.pad(w1, ((0, _EMB_PAD - HYENA_EMB_DIM), (0, 0))).T
    col = lambda a: a.reshape(HYENA_FILTER_HIDDEN, 1)
    w3d = w3.reshape(HYENA_FILTER_HIDDEN, 2, _FILT_W).transpose(1, 0, 2)
    min_decay = math.log(HYENA_DECAY_TARGET) / HYENA_SLOW_DECAY_PCT
    max_decay = math.log(HYENA_DECAY_TARGET) / HYENA_FAST_DECAY_PCT
    deltas = jnp.abs(jnp.linspace(min_decay, max_decay, HYENA_WIDTH, dtype=F32))[None, :]
    tile = min(512, n)
    tiles_per_dir = n // tile
    hv = pl.BlockSpec((HYENA_FILTER_HIDDEN, 1), lambda i: (0, 0))
    return pl.pallas_call(
        functools.partial(_filter_kernel, tiles_per_dir=tiles_per_dir),
        out_shape=[jax.ShapeDtypeStruct((2 * n, _FILT_W), F32),
                   jax.ShapeDtypeStruct((V7X_SUBLANES, _FILT_W), F32)],
        grid=(2 * tiles_per_dir,),
        in_specs=[pl.BlockSpec((_EMB_PAD, tile), lambda i: (0, i)),
                  pl.BlockSpec((tile, 1), lambda i: (i, 0)),
                  pl.BlockSpec((HYENA_FILTER_HIDDEN, _EMB_PAD), lambda i: (0, 0)), hv, hv,
                  pl.BlockSpec((HYENA_FILTER_HIDDEN, HYENA_FILTER_HIDDEN), lambda i: (0, 0)), hv, hv,
                  pl.BlockSpec((None, HYENA_FILTER_HIDDEN, _FILT_W), lambda i: (i // tiles_per_dir, 0, 0)),
                  pl.BlockSpec((1, HYENA_WIDTH), lambda i: (0, 0))],
        out_specs=[pl.BlockSpec((tile, _FILT_W), lambda i: (i, 0)),
                   pl.BlockSpec((V7X_SUBLANES, _FILT_W), lambda i: (0, 0))],
        compiler_params=_cparams(("arbitrary",), 32 << 20),
        name="hyena_filters",
    )(emb_t, t[:, None], w1t, col(b1), col(f1), w2.T, col(b2), col(f2), w3d, deltas)


def _fft_tables(n1):
    n2 = FFT_N2
    nb = n1 // 2 + 1
    nbp = -(-nb // V7X_SUBLANES) * V7X_SUBLANES
    k1 = np.arange(nbp)[:, None].astype(np.float64)
    valid = (np.arange(nbp) < nb)[:, None]
    s1 = np.arange(n1)[None, :]
    ang = 2.0 * np.pi * k1 * s1 / n1
    f1 = np.concatenate([np.where(valid, np.cos(ang), 0.0), np.where(valid, -np.sin(ang), 0.0)], axis=0)
    idx = np.arange(n2)
    ang2 = 2.0 * np.pi * np.outer(idx, idx) / n2
    f2 = np.stack([np.cos(ang2), -np.sin(ang2)])
    angt = 2.0 * np.pi * k1 * idx[None, :] / (n1 * n2)
    tw_f = np.stack([np.cos(angt), -np.sin(angt)])[:, :, None, :]
    t1 = np.arange(n1 // 2)[:, None]
    ang1 = 2.0 * np.pi * t1 * np.arange(nbp)[None, :] / n1
    inv1 = np.stack([np.cos(ang1), np.sin(ang1)])
    wk = np.where((np.arange(nbp) == 0) | (np.arange(nbp) == n1 // 2), 1.0, 2.0) * (np.arange(nbp) < nb)
    angb = 2.0 * np.pi * idx[:, None] * np.arange(nbp)[None, :] / (n1 * n2)
    tw_i = np.stack([np.cos(angb) * wk, np.sin(angb) * wk], axis=1) / (n1 * n2)
    as32 = lambda a: jnp.asarray(a, dtype=F32)
    return dict(nb=nb, nbp=nbp, f1=as32(f1), f2=as32(f2), tw_f=as32(tw_f), inv1=as32(inv1), tw_i=as32(tw_i))


def _split(a):
    hi = a.astype(BF16)
    return hi, (a - hi.astype(F32)).astype(BF16)


def _dot3(a_hi, a_lo, b):
    m = a_hi.shape[0]
    b_hi, b_lo = _split(b)
    t = jnp.dot(jnp.concatenate([a_hi, a_lo], axis=0), b_hi, preferred_element_type=F32)
    return t[:m] + t[m:] + jnp.dot(a_hi, b_lo, preferred_element_type=F32)


_FFT_T2 = 8


def _fft1_kernel(x_ref, f_ref, o_ref):
    nbp = o_ref.shape[1]
    f = f_ref[...]
    for j in range(_FFT_T2):
        a = jnp.dot(f, x_ref[:, j, :], preferred_element_type=F32, precision=HIGHEST)
        o_ref[0, :, j, :] = a[:nbp]
        o_ref[1, :, j, :] = a[nbp:]


def fft_level1(x, f1, n1):
    rows, ch = x.shape
    s1 = rows // FFT_N2
    nbp = f1.shape[0] // 2
    return pl.pallas_call(
        _fft1_kernel,
        out_shape=jax.ShapeDtypeStruct((2, nbp, FFT_N2, ch), F32),
        grid=(FFT_N2 // _FFT_T2,),
        in_specs=[pl.BlockSpec((s1, _FFT_T2, ch), lambda j: (0, j, 0)),
                  pl.BlockSpec((2 * nbp, s1), lambda j: (0, 0))],
        out_specs=pl.BlockSpec((2, nbp, _FFT_T2, ch), lambda j: (0, 0, j, 0)),
        compiler_params=_cparams(("parallel",), 2 * (s1 + 2 * nbp) * _FFT_T2 * ch * 4 + (16 << 20)),
        name="fft_level1",
    )(x.reshape(s1, FFT_N2, ch), f1[:, :s1])


def _cdot3(g_hi, g_lo, br, bi):
    m = g_hi.shape[0] // 2
    t1 = _dot3(g_hi, g_lo, br)
    t2 = _dot3(g_hi, g_lo, bi)
    return t1[:m] - t2[m:], t2[:m] + t1[m:]


def _twiddled_dft(f_ref, t_ref):
    fr, fi = f_ref[0], f_ref[1]
    tr, ti = t_ref[0], t_ref[1]
    return _split(jnp.concatenate([fr * tr - fi * ti, fr * ti + fi * tr], axis=0))


def _fft2_spectrum_kernel(a_ref, f_ref, t_ref, l1_ref, o_ref, *, nb):
    k1 = pl.program_id(0)

    @pl.when(k1 < nb)
    def _():
        g_hi, g_lo = _twiddled_dft(f_ref, t_ref)
        xr, xi = _cdot3(g_hi, g_lo, a_ref[0], a_ref[1])
        inv = 1.0 / l1_ref[0:1, :]
        o_ref[0] = xr * inv
        o_ref[1] = xi * inv

    @pl.when(k1 >= nb)
    def _():
        o_ref[...] = jnp.zeros(o_ref.shape, F32)


def fft_level2_spectrum(a, tabs, l1):
    _, nbp, n2, ch = a.shape
    blk = pl.BlockSpec((2, None, n2, ch), lambda k: (0, k, 0, 0))
    return pl.pallas_call(
        functools.partial(_fft2_spectrum_kernel, nb=tabs["nb"]),
        out_shape=jax.ShapeDtypeStruct(a.shape, F32),
        grid=(nbp,),
        in_specs=[blk, pl.BlockSpec((2, n2, n2), lambda k: (0, 0, 0)),
                  pl.BlockSpec((2, None, 1, n2), lambda k: (0, k, 0, 0)),
                  pl.BlockSpec((V7X_SUBLANES, ch), lambda k: (0, 0))],
        out_specs=blk,
        compiler_params=_cparams(("parallel",), 8 * 2 * n2 * ch * 4 + (8 << 20)),
        name="fft_level2_spectrum",
    )(a, tabs["f2"], tabs["tw_f"], l1)


def _fft2_conv_kernel(a_ref, kf_ref, f_ref, t_ref, o_ref, *, nb):
    k1 = pl.program_id(0)

    @pl.when(k1 < nb)
    def _():
        g_hi, g_lo = _twiddled_dft(f_ref, t_ref)
        xr, xi = _cdot3(g_hi, g_lo, a_ref[0], a_ref[1])
        kr, ki = kf_ref[0], kf_ref[1]
        yr = xr * kr - xi * ki
        yi = xr * ki + xi * kr
        n2 = f_ref.shape[1]
        f_hi, f_lo = _split(f_ref[...].reshape(2 * n2, n2))
        u1 = _dot3(f_hi, f_lo, yr)
        u2 = _dot3(f_hi, f_lo, yi)
        o_ref[0] = u1[:n2] + u2[n2:]
        o_ref[1] = u2[:n2] - u1[n2:]

    @pl.when(k1 >= nb)
    def _():
        o_ref[...] = jnp.zeros(o_ref.shape, F32)


def fft_level2_conv(a, kf, order, tabs):
    _, nbp, n2, ch = a.shape
    blk = pl.BlockSpec((2, None, n2, ch), lambda k: (0, k, 0, 0))
    return pl.pallas_call(
        functools.partial(_fft2_conv_kernel, nb=tabs["nb"]),
        out_shape=jax.ShapeDtypeStruct(a.shape, F32),
        grid=(nbp,),
        in_specs=[blk, pl.BlockSpec((2, None, n2, ch), lambda k: (0, k, 0, order)),
                  pl.BlockSpec((2, n2, n2), lambda k: (0, 0, 0)),
                  pl.BlockSpec((2, None, 1, n2), lambda k: (0, k, 0, 0))],
        out_specs=blk,
        compiler_params=_cparams(("parallel",), 10 * 2 * n2 * ch * 4 + (8 << 20)),
        name="fft_level2_conv",
    )(a, kf, tabs["f2"], tabs["tw_f"])


_INV_T2 = 8


def _ifft1_gate_kernel(b_ref, inv_ref, tw_ref, z_ref, g_ref, bias_ref, o_ref):
    ca, sa = inv_ref[0], inv_ref[1]
    for j in range(_INV_T2):
        cb, sb = tw_ref[j, 0:1, :], tw_ref[j, 1:2, :]
        mc = ca * cb - sa * sb
        ms = -(sa * cb + ca * sb)
        y = (jnp.dot(mc, b_ref[0, :, j, :], preferred_element_type=F32, precision=HIGHEST)
             + jnp.dot(ms, b_ref[1, :, j, :], preferred_element_type=F32, precision=HIGHEST))
        z = z_ref[:, j, :]
        o_ref[:, j, :] = (g_ref[:, j, :] * (y + z * bias_ref[...])).astype(o_ref.dtype)


def ifft_level1_gate(b, z, gate, bias, tabs, out_dtype):
    _, nbp, n2, ch = b.shape
    n = z.shape[0]
    s1 = n // n2
    view = lambda a: a.reshape(s1, n2, ch)
    seq = pl.BlockSpec((s1, _INV_T2, ch), lambda j: (0, j, 0))
    out = pl.pallas_call(
        _ifft1_gate_kernel,
        out_shape=jax.ShapeDtypeStruct((s1, n2, ch), out_dtype),
        grid=(n2 // _INV_T2,),
        in_specs=[pl.BlockSpec((2, nbp, _INV_T2, ch), lambda j: (0, 0, j, 0)),
                  pl.BlockSpec((2, s1, nbp), lambda j: (0, 0, 0)),
                  pl.BlockSpec((_INV_T2, 2, nbp), lambda j: (j, 0, 0)),
                  seq, seq, pl.BlockSpec((1, ch), lambda j: (0, 0))],
        out_specs=seq,
        compiler_params=_cparams(("parallel",), 2 * (2 * nbp + 3 * s1) * _INV_T2 * ch * 4 + (8 << 20)),
        name="ifft_level1_gate",
    )(b, tabs["inv1"], tabs["tw_i"], view(z), view(gate), bias)
    return out.reshape(n, ch)


def hyena_long_conv_fft(x1, x2, v, full, l1, bias):
    n, ch = v.shape
    n1 = 2 * n // FFT_N2
    tabs = _fft_tables(n1)
    kf = fft_level2_spectrum(fft_level1(full, tabs["f1"], n1), tabs, l1)
    z = v
    for o, gate in enumerate((x1, x2)):
        a = fft_level1(z, tabs["f1"], n1)
        b = fft_level2_conv(a, kf, o, tabs)
        last = o == HYENA_ORDER - 1
        z = ifft_level1_gate(b, z, gate, bias[o:o + 1], tabs, BF16 if last else F32)
    return z


def _direct_conv_kernel(lo_ref, hi_ref, l1_ref, z_ref, g_ref, bias_ref, o_ref, ebuf, *, n):
    inv = 1.0 / l1_ref[0:1, :]
    ebuf[0:n, :] = hi_ref[...] * inv
    ebuf[n:2 * n, :] = lo_ref[...] * inv
    for c0 in range(0, HYENA_WIDTH, V7X_LANES):
        cs = slice(c0, c0 + V7X_LANES)

        def body(g, acc, cs=cs):
            s0 = pl.multiple_of(g * V7X_SUBLANES, V7X_SUBLANES)
            base = pl.multiple_of(n - V7X_SUBLANES - s0, V7X_SUBLANES)
            win = ebuf[pl.ds(base, n + V7X_SUBLANES), cs]
            zb = z_ref[pl.ds(s0, V7X_SUBLANES), cs]
            for j in range(V7X_SUBLANES):
                acc = acc + win[V7X_SUBLANES - j:V7X_SUBLANES - j + n, :] * zb[j:j + 1, :]
            return acc

        y = lax.fori_loop(0, n // V7X_SUBLANES, body, jnp.zeros((n, V7X_LANES), F32))
        o_ref[:, cs] = (g_ref[:, cs] * (y + z_ref[:, cs] * bias_ref[:, cs])).astype(o_ref.dtype)


def hyena_long_conv_direct(x1, x2, v, full, l1, bias):
    n, ch = v.shape
    z = v
    for o, gate in enumerate((x1, x2)):
        last = o == HYENA_ORDER - 1
        seq = pl.BlockSpec((n, ch), lambda i: (0, 0))
        z = pl.pallas_call(
            functools.partial(_direct_conv_kernel, n=n),
            out_shape=jax.ShapeDtypeStruct((n, ch), BF16 if last else F32),
            grid=(1,),
            in_specs=[pl.BlockSpec((n, ch), lambda i, o=o: (0, o)),
                      pl.BlockSpec((n, ch), lambda i, o=o: (1, o)),
                      pl.BlockSpec((V7X_SUBLANES, ch), lambda i, o=o: (0, o)),
                      seq, seq, pl.BlockSpec((1, ch), lambda i: (0, 0))],
            out_specs=seq,
            scratch_shapes=[pltpu.VMEM((2 * n, ch), F32)],
            compiler_params=_cparams(("arbitrary",), 32 << 20),
            name="hyena_direct_conv",
        )(full, full, l1, z, gate, bias[o:o + 1])
    return z


_DIRECT_CONV_MAX = 512


def hyena_mix(p_c, short_w, short_b, w1, b1, f1, w2, b2, f2, w3, bias):
    n = p_c.shape[0]
    x1, x2, v = hyena_short_conv(p_c, short_w, short_b)
    full, l1 = hyena_filters(n, w1, b1, f1, w2, b2, f2, w3)
    if n <= _DIRECT_CONV_MAX:
        return hyena_long_conv_direct(x1, x2, v, full, l1, bias)
    return hyena_long_conv_fft(x1, x2, v, full, l1, bias)


def _out_kernel(cv_ref, at_ref, hy_ref, w_ref, x_ref, g_ref, gate_ref, o_ref):
    y = jnp.dot(cv_ref[...], w_ref[0:CONV_WIDTH, :], preferred_element_type=F32)
    y = y + jnp.dot(at_ref[...], w_ref[CONV_WIDTH:CONV_WIDTH + MLA_WIDTH, :], preferred_element_type=F32)
    y = y + jnp.dot(hy_ref[...], w_ref[CONV_WIDTH + MLA_WIDTH:, :], preferred_element_type=F32)
    o_ref[...] = x_ref[...] + gate_ref[...] * _rms(y, g_ref[...])


def out_projection(conv_o, attn, hy_o, w_out_bf16, x, g, gate):
    n, d = x.shape
    tm = min(512, n)
    mix = w_out_bf16.shape[0]
    vec = pl.BlockSpec((1, d), lambda i: (0, 0))
    row = lambda w: pl.BlockSpec((tm, w), lambda i: (i, 0))
    return pl.pallas_call(
        _out_kernel,
        out_shape=jax.ShapeDtypeStruct((n, d), F32),
        grid=(n // tm,),
        in_specs=[row(CONV_WIDTH), row(MLA_WIDTH), row(HYENA_WIDTH),
                  _resident((mix, d), lambda i: (0, 0)), row(d), vec, vec],
        out_specs=row(d),
        compiler_params=_cparams(("parallel",), mix * d * 2 + 6 * tm * d * 4 + (8 << 20)),
        name="out_projection",
    )(conv_o, attn, hy_o, w_out_bf16, x, g, gate)


def _ffn_kernel(x_ref, gpre_ref, sh_ref, sc_ref, w1_ref, w2_ref, gpost_ref, gate_ref, o_ref, h_sc):
    f = pl.program_id(1)

    @pl.when(f == 0)
    def _():
        h = _rms(x_ref[...], gpre_ref[...]) * (1.0 + sc_ref[...]) + sh_ref[...]
        h_sc[...] = h.astype(BF16)
        o_ref[...] = jnp.zeros(o_ref.shape, F32)

    a = jnp.dot(h_sc[...], w1_ref[...], preferred_element_type=F32)
    a = jnp.square(jnp.maximum(a, 0.0)).astype(BF16)
    o_ref[...] += jnp.dot(a, w2_ref[...], preferred_element_type=F32)

    @pl.when(f == pl.num_programs(1) - 1)
    def _():
        o_ref[...] = x_ref[...] + gate_ref[...] * _rms(o_ref[...], gpost_ref[...])


def ffn(x, g_pre, shift, scale, w_ff1, w_ff2, layer, g_post, gate):
    n, d = x.shape
    dff = w_ff1.shape[2]
    tm = min(512, n)
    tf = 1024
    vec = pl.BlockSpec((1, d), lambda i, f: (0, 0))
    vmem = 4 * tm * d * 4 + tm * d * 2 + 4 * d * tf * 2 + 3 * tm * tf * 4 + (8 << 20)
    return pl.pallas_call(
        _ffn_kernel,
        out_shape=jax.ShapeDtypeStruct((n, d), F32),
        grid=(n // tm, dff // tf),
        in_specs=[pl.BlockSpec((tm, d), lambda i, f: (i, 0)), vec, vec, vec,
                  pl.BlockSpec((None, d, tf), lambda i, f: (layer, 0, f)),
                  pl.BlockSpec((None, tf, d), lambda i, f: (layer, f, 0)), vec, vec],
        out_specs=pl.BlockSpec((tm, d), lambda i, f: (i, 0)),
        scratch_shapes=[pltpu.VMEM((tm, d), BF16)],
        compiler_params=_cparams(("parallel", "arbitrary"), vmem),
        name="ffn",
    )(x, g_pre, shift, scale, w_ff1, w_ff2, g_post, gate)


def _rot_cols(w):
    half = w.shape[-1] // 2
    return jnp.concatenate([-w[..., half:], w[..., :half]], axis=-1)


def _rope_table(rows):
    row = jnp.repeat(jnp.arange(rows, dtype=F32), GRID_W)
    col = jnp.tile(jnp.arange(GRID_W, dtype=F32), rows)
    axis_dim = QK_ROPE_DIM // 2
    inv = 1.0 / (ROPE_BASE ** (jnp.arange(0, axis_dim, 2, dtype=F32) / axis_dim))
    ang = jnp.concatenate([row[:, None] * inv, col[:, None] * inv], axis=-1)
    cos, sin = jnp.cos(ang), jnp.sin(ang)
    return jnp.concatenate([cos, cos, sin, sin], axis=-1)


def kernel(x, c, ctx, c_ctx, w_mod, b_mod, g_pre_mix, g_post_mix, g_pre_ffn, g_post_ffn, w_in, conv_dw_w, conv_dw_b, conv_ln_g, conv_ln_b, mla_q_norm, mla_w_uq, mla_kv_norm, mla_w_ukv, hy_short_w, hy_short_b, hy_w1, hy_b1, hy_freq1, hy_w2, hy_b2, hy_freq2, hy_w3, hy_bias, w_out, w_ff1, w_ff2):
    batch, seq, d = x.shape
    assert batch == 1 and c.shape[0] == 1 and ctx.shape[0] == 1
    depth = w_mod.shape[0]
    ctx_len = ctx.shape[1]
    xs, xc = x[0], ctx[0]

    mod = modulation(jnp.stack([c[0], c_ctx]), w_mod, b_mod)

    cs_lat = _rope_table(seq // GRID_W)
    cs_ctx = jnp.concatenate([jnp.ones((ctx_len, QK_ROPE_DIM), F32), jnp.zeros((ctx_len, QK_ROPE_DIM), F32)], -1)
    row = lambda a: a.reshape(1, -1)
    qa, qb_end = Q_LORA_RANK, Q_LORA_RANK + KV_LORA_RANK
    w_ff1_b, w_ff2_b = w_ff1.astype(BF16), w_ff2.astype(BF16)

    for l in range(depth):
        last = l == depth - 1
        mvec = lambda r, j: mod[l, r:r + 1, j * d:(j + 1) * d]
        wl = w_in[l]
        w_b = wl[:, IN_A:IN_A + IN_B]
        w_kr = w_b[:, qb_end:]
        w_in_p = jnp.concatenate([wl[:, :IN_A], wl[:, IN_A + IN_B:], w_b[:, :qb_end], w_kr, _rot_cols(w_kr)],
                                 axis=-1).astype(BF16)
        wq = mla_w_uq[l].reshape(Q_LORA_RANK, MLA_HEADS, QK_DIM).transpose(1, 0, 2)
        wq = jnp.concatenate([wq, _rot_cols(wq[..., QK_NOPE_DIM:])], axis=-1)
        wqt = wq.transpose(0, 2, 1).astype(BF16)
        wkv = mla_w_ukv[l].reshape(KV_LORA_RANK, MLA_HEADS, QK_NOPE_DIM + V_HEAD_DIM).transpose(1, 0, 2)
        wk = wkv[..., :QK_NOPE_DIM].astype(BF16)
        wvt = wkv[..., QK_NOPE_DIM:].transpose(0, 2, 1).astype(BF16)
        w_out_b = w_out[l].astype(BF16)

        def project(stream, r, cs):
            p_a, p_c, c_q, c_kv, kr = in_projection(stream, row(g_pre_mix[l]), mvec(r, 0), mvec(r, 1), w_in_p)
            qt, k, vt = mla_qkv(c_q, c_kv, kr, cs, cs.T, row(mla_q_norm[l]), row(mla_kv_norm[l]), wqt, wk, wvt)
            return p_a, p_c, qt, k, vt

        def finish(stream, r, p_a, p_c, attn):
            conv_o = conformer_conv(p_a, conv_dw_w[l], row(conv_dw_b[l]), row(conv_ln_g[l]), row(conv_ln_b[l]))
            hy_o = hyena_mix(p_c, hy_short_w[l], row(hy_short_b[l]), hy_w1[l], row(hy_b1[l]), row(hy_freq1[l]),
                             hy_w2[l], row(hy_b2[l]), row(hy_freq2[l]), hy_w3[l], hy_bias[l])
            y = out_projection(conv_o, attn, hy_o, w_out_b, stream, row(g_post_mix[l]), mvec(r, 2))
            return ffn(y, row(g_pre_ffn[l]), mvec(r, 3), mvec(r, 4), w_ff1_b, w_ff2_b, l, row(g_post_ffn[l]),
                       mvec(r, 5))

        pc_a, pc_c, qc, kc, vc = project(xc, 1, cs_ctx)
        p_a, p_c, q, k, v = project(xs, 0, cs_lat)
        attn = attention(q, [(kc, vc), (k, v)])
        new_x = finish(xs, 0, p_a, p_c, attn)
        if not last:
            xc = finish(xc, 1, pc_a, pc_c, attention(qc, [(kc, vc)]))
        xs = new_x
    return xs[None]
```
